```python
import jax, jax.numpy as jnp
from jax import lax
import numpy as np

D_MODEL = 1024
BATCH = 8
SEQ = 4096
DEPTH = 2

HEAD_DIM = 64
ROPE_THETA = 10000.0
RMS_EPS = 1e-6
MIX_HALF = D_MODEL // 2
A_HEADS = MIX_HALF // HEAD_DIM
A_KV_HEADS = 2
IDX_HEADS = 8
IDX_DIM = 64
DSA_TOPK = 256
Q_BLOCK = 128
B_HEAD_DIM = 128
B_HEADS = MIX_HALF // B_HEAD_DIM
B_CHUNK = 64
C_WIDTH = MIX_HALF
C_BLOCKS = 8
C_BLOCK_W = C_WIDTH // C_BLOCKS
CONV_WIDTH = 4
RG_C = 8.0
D_HEADS = MIX_HALF // HEAD_DIM
D_KV_HEADS = 2
WINDOW = 128
MLP_HIDDEN = 4 * D_MODEL
N_EVEN = (DEPTH + 1) // 2
N_ODD = DEPTH // 2

EVEN_SPLITS = (A_HEADS * HEAD_DIM, A_KV_HEADS * HEAD_DIM, A_KV_HEADS * HEAD_DIM,
               IDX_HEADS * IDX_DIM, IDX_DIM, IDX_HEADS,
               MIX_HALF, MIX_HALF, MIX_HALF, MIX_HALF)
ODD_SPLITS = (C_WIDTH, C_WIDTH, D_HEADS * HEAD_DIM, D_KV_HEADS * HEAD_DIM, D_KV_HEADS * HEAD_DIM)

kernel_name = 'hybrid_dsa_hgrn2_rglru_swa_trunk'


def rms_norm(x, g):
    xf = x.astype(jnp.float32)
    y = xf * lax.rsqrt(jnp.mean(xf * xf, axis=-1, keepdims=True) + RMS_EPS)
    return (y * g.astype(jnp.float32)).astype(x.dtype)


def split_cols(z, sizes):
    cuts = [int(c) for c in np.cumsum(sizes)[:-1]]
    return jnp.split(z, cuts, axis=-1)


def rope_tables(positions, dim):
    inv_freq = 1.0 / (ROPE_THETA ** (jnp.arange(0, dim, 2, dtype=jnp.float32) / dim))
    ang = positions.astype(jnp.float32)[..., None] * inv_freq
    return jnp.cos(ang)[:, :, None, :], jnp.sin(ang)[:, :, None, :]


def apply_rope(x, cos, sin):
    xf = x.astype(jnp.float32)
    x1, x2 = jnp.split(xf, 2, axis=-1)
    return jnp.concatenate([x1 * cos - x2 * sin, x2 * cos + x1 * sin], axis=-1).astype(x.dtype)


def dsa_attention(q, k, v, iq, ik, iw):
    b, s, h, dh = q.shape
    hkv = k.shape[2]
    grp = h // hkv
    topk = min(DSA_TOPK, s // 4)
    nblk = s // Q_BLOCK
    key_pos = jnp.arange(s)

    def blocks(a):
        return a.reshape(b, nblk, Q_BLOCK, *a.shape[2:]).swapaxes(0, 1)

    def block_fn(args):
        qb, iqb, iwb, qpos = args
        sc = jnp.einsum('bqhd,bsd->bqhs', iqb, ik).astype(jnp.float32)
        score = jnp.einsum('bqhs,bqh->bqs', jax.nn.relu(sc) * IDX_DIM ** -0.5,
                           iwb.astype(jnp.float32) * IDX_HEADS ** -0.5)
        causal = key_pos[None, :] <= qpos[:, None]
        score = jnp.where(causal[None], score, -jnp.inf)
        _, sel = lax.top_k(score, topk)
        valid = sel <= qpos[None, :, None]
        ksel = jax.vmap(lambda kb, ib: kb[ib])(k, sel)
        vsel = jax.vmap(lambda vb, ib: vb[ib])(v, sel)
        qg = qb.reshape(b, Q_BLOCK, hkv, grp, dh)
        logits = jnp.einsum('bqgrd,bqkgd->bqgrk', qg, ksel).astype(jnp.float32) * dh ** -0.5
        logits = jnp.where(valid[:, :, None, None, :], logits, -jnp.inf)
        p = jax.nn.softmax(logits, axis=-1)
        o = jnp.einsum('bqgrk,bqkgd->bqgrd', p.astype(vsel.dtype), vsel)
        return o.reshape(b, Q_BLOCK, h * dh)

    qpos_blocks = jnp.arange(s).reshape(nblk, Q_BLOCK)
    out = lax.map(block_fn, (blocks(q), blocks(iq), blocks(iw), qpos_blocks))
    return out.swapaxes(0, 1).reshape(b, s, h * dh)


def hgrn2(q, f_logit, i, lb):
    b, s, _ = q.shape
    nc = s // B_CHUNK
    qf = jax.nn.silu(q.astype(jnp.float32))
    f = lb + (1.0 - lb) * jax.nn.sigmoid(f_logit.astype(jnp.float32))
    kf = 1.0 - f
    logf = jnp.log(f)

    def chunks(a):
        return a.astype(jnp.float32).reshape(b, nc, B_CHUNK, B_HEADS, B_HEAD_DIM).transpose(1, 0, 3, 2, 4)

    qc, kc, vc = chunks(qf), chunks(kf), chunks(i)
    bc = jnp.cumsum(chunks(logf), axis=3)
    tri = jnp.tril(jnp.ones((B_CHUNK, B_CHUNK), dtype=bool))

    def step(state, xs):
        qt, kt, vt, bt = xs
        rel = jnp.where(tri[None, None, :, :, None],
                        bt[:, :, :, None, :] - bt[:, :, None, :, :], -jnp.inf)
        att = jnp.einsum('bhtd,bhsd,bhtsd->bhts', qt, kt, jnp.exp(rel))
        o = jnp.einsum('bhts,bhsv->bhtv', att, vt) + \
            jnp.einsum('bhtd,bhdv->bhtv', qt * jnp.exp(bt), state)
        b_last = bt[:, :, -1:, :]
        new_state = jnp.exp(b_last[:, :, 0, :])[..., None] * state + \
            jnp.einsum('bhsd,bhsv->bhdv', kt * jnp.exp(b_last - bt), vt)
        return new_state, o

    state0 = jnp.zeros((b, B_HEADS, B_HEAD_DIM, B_HEAD_DIM), jnp.float32)
    _, o = lax.scan(step, state0, (qc, kc, vc, bc))
    return o.transpose(1, 0, 3, 2, 4).reshape(b, s, B_HEADS, B_HEAD_DIM)


def causal_depthwise_conv(x, w, bias):
    y = lax.conv_general_dilated(x, w[:, None, :].astype(x.dtype), window_strides=(1,),
                                 padding=[(CONV_WIDTH - 1, 0)],
                                 dimension_numbers=('NWC', 'WIO', 'NWC'),
                                 feature_group_count=x.shape[-1])
    return y + bias.astype(x.dtype)


def rg_lru(x, wr, br, wi, bi, lam):
    b, s, c = x.shape
    xf = x.astype(jnp.float32)
    xb = xf.reshape(b, s, C_BLOCKS, C_BLOCK_W)
    r = jax.nn.sigmoid(jnp.einsum('bsnc,ncd->bsnd', xb, wr.astype(jnp.float32)).reshape(b, s, c)
                       + br.astype(jnp.float32))
    ig = jax.nn.sigmoid(jnp.einsum('bsnc,ncd->bsnd', xb, wi.astype(jnp.float32)).reshape(b, s, c)
                        + bi.astype(jnp.float32))
    log_a = -RG_C * r * jax.nn.softplus(-lam.astype(jnp.float32))
    a = jnp.exp(log_a)
    u = jnp.sqrt(-jnp.expm1(2.0 * log_a)) * (ig * xf)

    def combine(left, right):
        a1, b1 = left
        a2, b2 = right
        return a1 * a2, a2 * b1 + b2

    _, h = lax.associative_scan(combine, (a, u), axis=1)
    return h.astype(x.dtype)


def swa_with_sinks(q, k, v, sinks):
    b, s, h, dh = q.shape
    hkv = k.shape[2]
    grp = h // hkv
    nb = s // WINDOW
    qb = q.reshape(b, nb, WINDOW, hkv, grp, dh)

    def with_prev(a):
        ab = a.reshape(b, nb, WINDOW, hkv, dh)
        prev = jnp.concatenate([jnp.zeros_like(ab[:, :1]), ab[:, :-1]], axis=1)
        return jnp.concatenate([prev, ab], axis=2)

    kk, vv = with_prev(k), with_prev(v)
    logits = jnp.einsum('bnqgrd,bnkgd->bngrqk', qb, kk).astype(jnp.float32) * dh ** -0.5
    qi = jnp.arange(WINDOW)[:, None]
    kj = jnp.arange(2 * WINDOW)[None, :]
    diff = WINDOW + qi - kj
    band = (diff >= 0) & (diff < WINDOW)
    key_abs = jnp.arange(nb)[:, None, None] * WINDOW + kj[None] - WINDOW
    mask = band[None] & (key_abs >= 0)
    logits = jnp.where(mask[None, :, None, None], logits, -jnp.inf)
    sink = sinks.astype(jnp.float32).reshape(hkv, grp)[None, None, :, :, None, None]
    m = jnp.maximum(jnp.max(logits, axis=-1, keepdims=True), sink)
    p = jnp.exp(logits - m)
    p = p / (jnp.sum(p, axis=-1, keepdims=True) + jnp.exp(sink - m))
    o = jnp.einsum('bngrqk,bnkgd->bnqgrd', p.astype(vv.dtype), vv)
    return o.reshape(b, s, h * dh)


def setup_inputs(seed: int = 0) -> dict:
    key = jax.random.key(seed)
    ks = jax.random.split(key, 26)
    f32 = jnp.float32

    def nrm(k, shape, scale):
        return jax.random.normal(k, shape, f32) * scale

    x = jax.random.normal(ks[0], (BATCH, SEQ, D_MODEL), f32)
    offsets = jax.random.randint(ks[1], (BATCH, 1), 0, 2048, dtype=jnp.int32)
    positions = offsets + jnp.arange(SEQ, dtype=jnp.int32)[None, :]
    lam_target = jax.random.uniform(ks[17], (N_ODD, C_WIDTH), f32, minval=0.9, maxval=0.999)
    lam_base = lam_target ** (1.0 / RG_C)
    return {
        'x': x,
        'positions': positions,
        'norm_mix_g': 1.0 + nrm(ks[2], (DEPTH, D_MODEL), 0.05),
        'norm_mlp_g': 1.0 + nrm(ks[3], (DEPTH, D_MODEL), 0.05),
        'even_w_in': nrm(ks[4], (N_EVEN, D_MODEL, sum(EVEN_SPLITS)), D_MODEL ** -0.5),
        'even_w_out': nrm(ks[5], (N_EVEN, D_MODEL, D_MODEL), D_MODEL ** -0.5),
        'a_q_norm_g': 1.0 + nrm(ks[6], (N_EVEN, HEAD_DIM), 0.05),
        'a_k_norm_g': 1.0 + nrm(ks[7], (N_EVEN, HEAD_DIM), 0.05),
        'b_lb_logits': nrm(ks[8], (N_EVEN + 1, MIX_HALF), 0.5),
        'b_out_norm_g': 1.0 + nrm(ks[9], (N_EVEN, B_HEAD_DIM), 0.05),
        'odd_w_in': nrm(ks[10], (N_ODD, D_MODEL, sum(ODD_SPLITS)), D_MODEL ** -0.5),
        'odd_w_out': nrm(ks[11], (N_ODD, D_MODEL, D_MODEL), D_MODEL ** -0.5),
        'c_conv_w': nrm(ks[12], (N_ODD, CONV_WIDTH, C_WIDTH), CONV_WIDTH ** -0.5),
        'c_conv_b': nrm(ks[13], (N_ODD, C_WIDTH), 0.01),
        'c_rgate_w': nrm(ks[14], (N_ODD, C_BLOCKS, C_BLOCK_W, C_BLOCK_W), C_BLOCK_W ** -0.5),
        'c_rgate_b': nrm(ks[15], (N_ODD, C_WIDTH), 0.01),
        'c_igate_w': nrm(ks[16], (N_ODD, C_BLOCKS, C_BLOCK_W, C_BLOCK_W), C_BLOCK_W ** -0.5),
        'c_igate_b': nrm(ks[18], (N_ODD, C_WIDTH), 0.01),
        'c_lambda': jnp.log(lam_base) - jnp.log1p(-lam_base),
        'd_q_norm_g': 1.0 + nrm(ks[19], (N_ODD, HEAD_DIM), 0.05),
        'd_k_norm_g': 1.0 + nrm(ks[20], (N_ODD, HEAD_DIM), 0.05),
        'd_sinks': nrm(ks[21], (N_ODD, D_HEADS), 0.5),
        'mlp_w_up': nrm(ks[22], (DEPTH, D_MODEL, MLP_HIDDEN), D_MODEL ** -0.5),
        'mlp_w_down': nrm(ks[23], (DEPTH, MLP_HIDDEN, D_MODEL), MLP_HIDDEN ** -0.5),
    }


def reference(x, positions, norm_mix_g, norm_mlp_g, even_w_in, even_w_out, a_q_norm_g, a_k_norm_g,
              b_lb_logits, b_out_norm_g, odd_w_in, odd_w_out, c_conv_w, c_conv_b, c_rgate_w, c_rgate_b,
              c_igate_w, c_igate_b, c_lambda, d_q_norm_g, d_k_norm_g, d_sinks, mlp_w_up, mlp_w_down):
    b, s, _ = x.shape
    cos, sin = rope_tables(positions, HEAD_DIM)
    cos_i, sin_i = rope_tables(positions, IDX_DIM)
    lb_all = jnp.cumsum(jax.nn.softmax(b_lb_logits.astype(jnp.float32), axis=0), axis=0)

    for layer in range(DEPTH):
        h = rms_norm(x, norm_mix_g[layer])
        if layer % 2 == 0:
            j = layer // 2
            z = h @ even_w_in[j]
            aq, ak, av, iq, ik, iw, bq, bf, bi, bg = split_cols(z, EVEN_SPLITS)
            aq = apply_rope(rms_norm(aq.reshape(b, s, A_HEADS, HEAD_DIM), a_q_norm_g[j]), cos, sin)
            ak = apply_rope(rms_norm(ak.reshape(b, s, A_KV_HEADS, HEAD_DIM), a_k_norm_g[j]), cos, sin)
            av = av.reshape(b, s, A_KV_HEADS, HEAD_DIM)
            iq = apply_rope(iq.reshape(b, s, IDX_HEADS, IDX_DIM), cos_i, sin_i)
            ik = apply_rope(ik[:, :, None, :], cos_i, sin_i)[:, :, 0, :]
            o_a = dsa_attention(aq, ak, av, iq, ik, iw)
            o_b = rms_norm(hgrn2(bq, bf, bi, lb_all[j]), b_out_norm_g[j]).reshape(b, s, MIX_HALF)
            o_b = (o_b * jax.nn.silu(bg.astype(jnp.float32))).astype(x.dtype)
            mix = jnp.concatenate([o_a.astype(x.dtype), o_b], axis=-1) @ even_w_out[j]
        else:
            j = layer // 2
            z = h @ odd_w_in[j]
            cg, cx, dq, dk, dv = split_cols(z, ODD_SPLITS)
            xc = causal_depthwise_conv(cx, c_conv_w[j], c_conv_b[j])
            o_c = jax.nn.gelu(cg) * rg_lru(xc, c_rgate_w[j], c_rgate_b[j], c_igate_w[j], c_igate_b[j], c_lambda[j])
            dq = apply_rope(rms_norm(dq.reshape(b, s, D_HEADS, HEAD_DIM), d_q_norm_g[j]), cos, sin)
            dk = apply_rope(rms_norm(dk.reshape(b, s, D_KV_HEADS, HEAD_DIM), d_k_norm_g[j]), cos, sin)
            dv = dv.reshape(b, s, D_KV_HEADS, HEAD_DIM)
            o_d = swa_with_sinks(dq, dk, dv, d_sinks[j])
            mix = jnp.concatenate([o_c.astype(x.dtype), o_d.astype(x.dtype)], axis=-1) @ odd_w_out[j]
        x = x + mix
        h = rms_norm(x, norm_mlp_g[layer])
        x = x + jnp.square(jax.nn.relu(h @ mlp_w_up[layer])) @ mlp_w_down[layer]
    return x
```

```python
import functools
import math

import numpy as np
import jax
import jax.numpy as jnp
from jax import lax
from jax.experimental import pallas as pl
from jax.experimental.pallas import tpu as pltpu

F32 = jnp.float32
BF16 = jnp.bfloat16

HEAD_DIM = 64
ROPE_THETA = 10000.0
RMS_EPS = 1e-6
KV_HEADS = 2
Q_HEADS = 8
GROUP = Q_HEADS // KV_HEADS
IDX_HEADS = 8
DSA_TOPK = 256
BLK = 128
B_HEAD_DIM = 128
B_HEADS = 4
B_CHUNK = 64
B_SUB = 16
CONV_WIDTH = 4
RG_C = 8.0
C_BLOCKS = 8
NEG_BIG = -1e30
INT_MIN = -2 ** 31

VMEM_LIMIT = 56 * 1024 * 1024


def _cparams(sem):
    return pltpu.CompilerParams(dimension_semantics=sem, vmem_limit_bytes=VMEM_LIMIT)


def _rope_kernel(pos_ref, invf_ref, cos_ref, sin_ref):
    ang = pos_ref[0].astype(F32) * invf_ref[...]
    cos_ref[0] = jnp.cos(ang)
    sin_ref[0] = jnp.sin(ang)


def rope_tables_t(positions):
    b, s = positions.shape
    half = HEAD_DIM // 2
    invf = (1.0 / (ROPE_THETA ** (jnp.arange(0, HEAD_DIM, 2, dtype=F32) / HEAD_DIM))).reshape(half, 1)
    out = jax.ShapeDtypeStruct((b, half, s), F32)
    return pl.pallas_call(
        _rope_kernel,
        grid=(b,),
        in_specs=[pl.BlockSpec((1, 1, s), lambda i: (i, 0, 0)),
                  pl.BlockSpec((half, 1), lambda i: (0, 0))],
        out_specs=[pl.BlockSpec((1, half, s), lambda i: (i, 0, 0))] * 2,
        out_shape=[out, out],
        compiler_params=_cparams(("arbitrary",)),
        name="rope_tables",
    )(positions.reshape(b, 1, s), invf)


def _head_norm_rope_t(blk, gain_col, cos, sin, scale):
    if gain_col is not None:
        ms = jnp.mean(blk * blk, axis=0, keepdims=True)
        blk = (blk * lax.rsqrt(ms + RMS_EPS)) * gain_col
    half = HEAD_DIM // 2
    x1, x2 = blk[:half], blk[half:]
    o1 = x1 * cos - x2 * sin
    o2 = x2 * cos + x1 * sin
    out = jnp.concatenate([o1, o2], axis=0)
    if scale != 1.0:
        out = out * scale
    return out


def _proj_kernel(*refs, tm, with_indexer, n_nat):
    if with_indexer:
        (x_ref, g_ref, wt_ref, wn_ref, cos_ref, sin_ref, qgain_ref, kgain_ref,
         qg_ref, k_ref, vt_ref, iq_ref, ik_ref, iw_ref, nat_ref) = refs
    else:
        (x_ref, g_ref, wt_ref, wn_ref, cos_ref, sin_ref, qgain_ref, kgain_ref,
         qg_ref, k_ref, vt_ref, nat_ref) = refs
    nq = tm // BLK
    x = x_ref[0]
    ms = jnp.mean(x * x, axis=-1, keepdims=True)
    hn = ((x * lax.rsqrt(ms + RMS_EPS)) * g_ref[...]).astype(BF16)
    zt = lax.dot_general(wt_ref[...], hn, (((1,), (1,)), ((), ())), preferred_element_type=F32)
    nat_ref[0] = jnp.dot(hn, wn_ref[...], preferred_element_type=F32)
    cos = cos_ref[0]
    sin = sin_ref[0]
    qgain = qgain_ref[...]
    kgain = kgain_ref[...]
    zeros64 = jnp.zeros((HEAD_DIM, BLK), BF16)
    r = 0
    for h in range(Q_HEADS):
        g, hh = divmod(h, GROUP)
        q = _head_norm_rope_t(zt[r:r + HEAD_DIM], qgain, cos, sin, HEAD_DIM ** -0.5).astype(BF16)
        r += HEAD_DIM
        for j in range(nq):
            qg_ref[0, j, g * HEAD_DIM:(g + 1) * HEAD_DIM, h * BLK:(h + 1) * BLK] = q[:, j * BLK:(j + 1) * BLK]
            qg_ref[0, j, (1 - g) * HEAD_DIM:(2 - g) * HEAD_DIM, h * BLK:(h + 1) * BLK] = zeros64
    ks = []
    for g in range(KV_HEADS):
        ks.append(_head_norm_rope_t(zt[r:r + HEAD_DIM], kgain, cos, sin, 1.0))
        r += HEAD_DIM
    k_ref[0] = jnp.concatenate(ks, axis=0).T.astype(BF16)
    vt_ref[0] = zt[r:r + KV_HEADS * HEAD_DIM].astype(BF16)
    r += KV_HEADS * HEAD_DIM
    if with_indexer:
        for h in range(IDX_HEADS):
            iq = _head_norm_rope_t(zt[r:r + HEAD_DIM], None, cos, sin, 1.0).astype(BF16)
            r += HEAD_DIM
            for j in range(nq):
                iq_ref[0, j, 0:HEAD_DIM, h * BLK:(h + 1) * BLK] = iq[:, j * BLK:(j + 1) * BLK]
                iq_ref[0, j, HEAD_DIM:2 * HEAD_DIM, h * BLK:(h + 1) * BLK] = zeros64
        ikt = _head_norm_rope_t(zt[r:r + HEAD_DIM], None, cos, sin, 1.0)
        r += HEAD_DIM
        ik_ref[0] = jnp.concatenate([ikt, jnp.zeros_like(ikt)], axis=0).T.astype(BF16)
        iw_ref[0] = zt[r:r + IDX_HEADS] * (HEAD_DIM ** -0.5 * IDX_HEADS ** -0.5)
        r += IDX_HEADS


def input_projection(x, gain, w_t, w_n, cos_t, sin_t, q_gain, k_gain, *, with_indexer, tm=512):
    b, s, d = x.shape
    tm = min(tm, s)
    nt = w_t.shape[0]
    n_nat = w_n.shape[1]
    nqb = s // BLK
    half = HEAD_DIM // 2
    grid = (b, s // tm)
    in_specs = [
        pl.BlockSpec((1, tm, d), lambda i, j: (i, j, 0)),
        pl.BlockSpec((1, d), lambda i, j: (0, 0)),
        pl.BlockSpec((nt, d), lambda i, j: (0, 0)),
        pl.BlockSpec((d, n_nat), lambda i, j: (0, 0)),
        pl.BlockSpec((1, half, tm), lambda i, j: (i, 0, j)),
        pl.BlockSpec((1, half, tm), lambda i, j: (i, 0, j)),
        pl.BlockSpec((HEAD_DIM, 1), lambda i, j: (0, 0)),
        pl.BlockSpec((HEAD_DIM, 1), lambda i, j: (0, 0)),
    ]
    out_shape = [
        jax.ShapeDtypeStruct((b, nqb, BLK, Q_HEADS * BLK), BF16),
        jax.ShapeDtypeStruct((b, s, KV_HEADS * HEAD_DIM), BF16),
        jax.ShapeDtypeStruct((b, KV_HEADS * HEAD_DIM, s), BF16),
    ]
    out_specs = [
        pl.BlockSpec((1, tm // BLK, BLK, Q_HEADS * BLK), lambda i, j: (i, j, 0, 0)),
        pl.BlockSpec((1, tm, KV_HEADS * HEAD_DIM), lambda i, j: (i, j, 0)),
        pl.BlockSpec((1, KV_HEADS * HEAD_DIM, tm), lambda i, j: (i, 0, j)),
    ]
    if with_indexer:
        out_shape += [
            jax.ShapeDtypeStruct((b, nqb, BLK, IDX_HEADS * BLK), BF16),
            jax.ShapeDtypeStruct((b, s, BLK), BF16),
            jax.ShapeDtypeStruct((b, IDX_HEADS, s), F32),
        ]
        out_specs += [
            pl.BlockSpec((1, tm // BLK, BLK, IDX_HEADS * BLK), lambda i, j: (i, j, 0, 0)),
            pl.BlockSpec((1, tm, BLK), lambda i, j: (i, j, 0)),
            pl.BlockSpec((1, IDX_HEADS, tm), lambda i, j: (i, 0, j)),
        ]
    out_shape.append(jax.ShapeDtypeStruct((b, s, n_nat), F32))
    out_specs.append(pl.BlockSpec((1, tm, n_nat), lambda i, j: (i, j, 0)))
    return pl.pallas_call(
        functools.partial(_proj_kernel, tm=tm, with_indexer=with_indexer, n_nat=n_nat),
        grid=grid,
        in_specs=in_specs,
        out_specs=out_specs,
        out_shape=out_shape,
        compiler_params=_cparams(("arbitrary", "arbitrary")),
        name="in_proj_dsa" if with_indexer else "in_proj_swa",
    )(x, gain.reshape(1, d), w_t, w_n, cos_t, sin_t, q_gain.reshape(HEAD_DIM, 1), k_gain.reshape(HEAD_DIM, 1))


PAIR = 2 * BLK
QUAD = 4 * BLK
COUNT_ROWS = 64
FLT_LOWEST = float(np.finfo(np.float32).min)
SEARCH_FIRST_STEPS = 12
SEARCH_STEPS_PER_CHECK = 4
SEARCH_MAX_STEPS = 64


def _fold8(x):
    return x.reshape(x.shape[0] // 8, 8, x.shape[1])


def _pair_loop(npair, pair_fn, carry):
    def two(i, c):
        return pair_fn(2 * i + 1, pair_fn(2 * i, c))
    carry = lax.fori_loop(0, npair // 2, two, carry)
    return lax.fori_loop(2 * (npair // 2), npair, pair_fn, carry)


def _count(sc_ref, nquad, thr):
    def body(i, acc):
        off = pl.multiple_of(i * QUAD, QUAD)
        hit = jnp.where(sc_ref[pl.ds(off, QUAD), :] >= thr, 1.0, 0.0)
        return acc + jnp.sum(hit.reshape(QUAD // COUNT_ROWS, COUNT_ROWS, BLK), axis=0)
    acc = lax.fori_loop(0, nquad, body, jnp.zeros((COUNT_ROWS, BLK), F32))
    return jnp.sum(acc, axis=0, keepdims=True)


def _drop_excess(sc_ref, nquad, thr, excess, before):
    def cond(ex):
        return jnp.max(ex) > 0.0

    def body(ex):
        def min_body(i, acc):
            off = pl.multiple_of(i * QUAD, QUAD)
            blk = sc_ref[pl.ds(off, QUAD), :]
            return jnp.minimum(acc, jnp.min(_fold8(jnp.where(blk >= thr, blk, jnp.inf)), axis=0))
        low = jnp.min(lax.fori_loop(0, nquad, min_body, jnp.full((8, BLK), jnp.inf, F32)), axis=0, keepdims=True)

        def cnt_body(i, acc):
            off = pl.multiple_of(i * QUAD, QUAD)
            return acc + jnp.sum(_fold8(jnp.where(sc_ref[pl.ds(off, QUAD), :] == low, 1.0, 0.0)), axis=0)
        copies = jnp.sum(lax.fori_loop(0, nquad, cnt_body, jnp.zeros((8, BLK), F32)), axis=0, keepdims=True)
        drop = jnp.where(ex > 0.0, jnp.minimum(copies, ex), 0.0)
        keep = copies - drop

        def drop_body(kb, seen):
            off = pl.multiple_of(kb * BLK, BLK)
            blk = sc_ref[pl.ds(off, BLK), :]
            eq = blk == low
            eqf = jnp.where(eq, 1.0, 0.0)
            rank = jnp.dot(before, eqf.astype(BF16), preferred_element_type=F32) + seen
            sc_ref[pl.ds(off, BLK), :] = jnp.where(eq & (rank >= keep), -jnp.inf, blk)
            return seen + jnp.sum(eqf, axis=0, keepdims=True)
        lax.fori_loop(0, 4 * nquad, drop_body, jnp.zeros((1, BLK), F32))
        return ex - drop

    lax.while_loop(cond, body, excess)


def _search_step(sc_ref, nquad, topk, st):
    lo, hi, c_lo, c_hi, done = st
    guess = lo + (hi - lo) * 0.5
    inside = (guess > lo) & (guess < hi)
    c = _count(sc_ref, nquad, guess)
    move = inside & (done == 0.0)
    up = move & (c >= topk)
    down = move & (c < topk)
    lo = jnp.where(up, guess, lo)
    c_lo = jnp.where(up, c, c_lo)
    hi = jnp.where(down, guess, hi)
    c_hi = jnp.where(down, c, c_hi)
    done = jnp.where(inside & (c_lo != topk), done, 1.0)
    return lo, hi, c_lo, c_hi, done


def _dsa_kernel(q_ref, k_ref, vt_ref, iq_ref, ik_ref, iw_ref, o_ref, sc_ref, lg_ref, acc_ref, *, topk):
    qi = pl.program_id(1)
    npair = (qi + 2) // 2
    iq = iq_ref[0, 0]
    w = iw_ref[0]
    nquad = (qi + 4) // 4
    row2 = lax.broadcasted_iota(jnp.int32, (PAIR, BLK), 0)
    col2 = lax.broadcasted_iota(jnp.int32, (PAIR, BLK), 1)

    def score_body(pi, carry):
        lo8, hi8 = carry
        off = pl.multiple_of(pi * PAIR, PAIR)
        ikb = ik_ref[0, pl.ds(off, PAIR), :]
        tot = jnp.zeros((PAIR, BLK), F32)
        for h in range(IDX_HEADS):
            sc = jnp.dot(ikb, iq_ref[0, 0, :, h * BLK:(h + 1) * BLK], preferred_element_type=F32)
            tot = tot + jnp.maximum(sc, 0.0) * w[h:h + 1, :]
        causal = (pi * PAIR + row2) <= (qi * BLK + col2)
        sc_ref[pl.ds(off, PAIR), :] = jnp.where(causal, tot, -jnp.inf)
        lo8 = jnp.minimum(lo8, jnp.min(_fold8(jnp.where(causal, tot, jnp.inf)), axis=0))
        hi8 = jnp.maximum(hi8, jnp.max(_fold8(jnp.where(causal, tot, -jnp.inf)), axis=0))
        return lo8, hi8

    lo8, hi8 = _pair_loop(npair, score_body,
                          (jnp.full((8, BLK), jnp.inf, F32), jnp.full((8, BLK), -jnp.inf, F32)))

    @pl.when(2 * npair < 4 * nquad)
    def _():
        sc_ref[pl.ds(pl.multiple_of(npair * PAIR, PAIR), PAIR), :] = jnp.full((PAIR, BLK), -jnp.inf, F32)

    smin = jnp.min(lo8, axis=0, keepdims=True)
    smax = jnp.max(hi8, axis=0, keepdims=True)
    n_valid = (qi * BLK + 1 + lax.broadcasted_iota(jnp.int32, (1, BLK), 1)).astype(F32)
    few = n_valid <= topk
    above = smax + (jnp.abs(smax) * 1e-6 + 1e-30)
    state = (smin, above, n_valid, jnp.zeros((1, BLK), F32), jnp.where(few, 1.0, 0.0))

    def search_cond(st):
        i, inner = st
        return (i < SEARCH_MAX_STEPS) & (jnp.min(inner[4]) == 0.0)

    def search_body(st):
        i, inner = st
        for _ in range(SEARCH_STEPS_PER_CHECK):
            inner = _search_step(sc_ref, nquad, topk, inner)
        return i + SEARCH_STEPS_PER_CHECK, inner

    _, state = lax.fori_loop(0, SEARCH_FIRST_STEPS // SEARCH_STEPS_PER_CHECK, lambda i, st: search_body(st),
                             (jnp.int32(0), state))
    _, (lo, _, c_lo, _, _) = lax.while_loop(search_cond, search_body, (jnp.int32(SEARCH_FIRST_STEPS), state))
    thr = jnp.where(few, FLT_LOWEST, lo)
    excess = jnp.where(few, 0.0, c_lo - topk)

    @pl.when(jnp.max(excess) > 0.0)
    def _():
        row = lax.broadcasted_iota(jnp.int32, (BLK, BLK), 0)
        col = lax.broadcasted_iota(jnp.int32, (BLK, BLK), 1)
        _drop_excess(sc_ref, nquad, thr, excess, (col < row).astype(BF16))

    q_all = q_ref[0, 0]

    def logit_body(pi, mx):
        off = pl.multiple_of(pi * PAIR, PAIR)
        kb = k_ref[0, pl.ds(off, PAIR), :]
        bias = jnp.where(sc_ref[pl.ds(off, PAIR), :] >= thr, 0.0, NEG_BIG)
        parts = []
        for h in range(Q_HEADS):
            lgh = jnp.dot(kb, q_ref[0, 0, :, h * BLK:(h + 1) * BLK], preferred_element_type=F32) + bias
            lg_ref[pl.ds(off, PAIR), h * BLK:(h + 1) * BLK] = lgh
            parts.append(jnp.max(_fold8(lgh), axis=0))
        return jnp.maximum(mx, jnp.concatenate(parts, axis=1))

    mx = _pair_loop(npair, logit_body, jnp.full((8, Q_HEADS * BLK), NEG_BIG, F32))
    m = jnp.max(mx, axis=0, keepdims=True)

    acc_ref[...] = jnp.zeros(acc_ref.shape, F32)

    def pv_body(pi, ls):
        off = pl.multiple_of(pi * PAIR, PAIR)
        sums, pbs = [], []
        for h in range(Q_HEADS):
            cs = slice(h * BLK, (h + 1) * BLK)
            p = jnp.exp(lg_ref[pl.ds(off, PAIR), cs] - m[:, cs])
            sums.append(jnp.sum(_fold8(p), axis=0))
            pbs.append(p.astype(BF16))
        for g in range(KV_HEADS):
            pb = jnp.concatenate(pbs[g * GROUP:(g + 1) * GROUP], axis=1)
            vt = vt_ref[0, g * HEAD_DIM:(g + 1) * HEAD_DIM, pl.ds(off, PAIR)]
            acc_ref[g] += jnp.dot(vt, pb, preferred_element_type=F32)
        return ls + jnp.concatenate(sums, axis=1)

    ls = _pair_loop(npair, pv_body, jnp.zeros((8, Q_HEADS * BLK), F32))
    l = jnp.sum(ls, axis=0, keepdims=True)

    for pair in range(Q_HEADS // 2):
        g, hh = divmod(2 * pair, GROUP)
        o = acc_ref[g] / l[:, g * GROUP * BLK:(g + 1) * GROUP * BLK]
        two = jnp.concatenate([o[:, hh * BLK:(hh + 1) * BLK], o[:, (hh + 1) * BLK:(hh + 2) * BLK]], axis=0)
        o_ref[0, :, pair * BLK:(pair + 1) * BLK] = two.T.astype(o_ref.dtype)


def dsa_attention(qg, k, vt, iq, ik, iw):
    b, s, _ = k.shape
    assert s % QUAD == 0, "the counting loops pad the score tile to whole groups of four key blocks"
    nqb = s // BLK
    topk = min(DSA_TOPK, s // 4)
    return pl.pallas_call(
        functools.partial(_dsa_kernel, topk=topk),
        grid=(b, nqb),
        in_specs=[
            pl.BlockSpec((1, 1, BLK, Q_HEADS * BLK), lambda i, j: (i, j, 0, 0)),
            pl.BlockSpec((1, s, KV_HEADS * HEAD_DIM), lambda i, j: (i, 0, 0)),
            pl.BlockSpec((1, KV_HEADS * HEAD_DIM, s), lambda i, j: (i, 0, 0)),
            pl.BlockSpec((1, 1, BLK, IDX_HEADS * BLK), lambda i, j: (i, j, 0, 0)),
            pl.BlockSpec((1, s, BLK), lambda i, j: (i, 0, 0)),
            pl.BlockSpec((1, IDX_HEADS, BLK), lambda i, j: (i, 0, j)),
        ],
        out_specs=pl.BlockSpec((1, BLK, Q_HEADS * HEAD_DIM), lambda i, j: (i, j, 0)),
        out_shape=jax.ShapeDtypeStruct((b, s, Q_HEADS * HEAD_DIM), BF16),
        scratch_shapes=[
            pltpu.VMEM((s, BLK), F32),
            pltpu.VMEM((s, Q_HEADS * BLK), F32),
            pltpu.VMEM((KV_HEADS, HEAD_DIM, GROUP * BLK), F32),
        ],
        compiler_params=_cparams(("arbitrary", "arbitrary")),
        name="dsa_attention",
    )(qg, k, vt, iq, ik, iw)


def _split3(x):
    hi = x.astype(BF16)
    r1 = x - hi.astype(F32)
    mid = r1.astype(BF16)
    lo = (r1 - mid.astype(F32)).astype(BF16)
    return hi, mid, lo


def _hgrn_chunk(qraw, flog, v, lb, state_t):
    c = B_CHUNK
    q = qraw * jax.nn.sigmoid(qraw)
    f = lb + (1.0 - lb) * jax.nn.sigmoid(flog)
    k = 1.0 - f
    logf = jnp.log(f)
    row = lax.broadcasted_iota(jnp.int32, (c, c), 0)
    lane = lax.broadcasted_iota(jnp.int32, (c, c), 1)
    incl = (lane <= row).astype(BF16)
    b = sum(jnp.dot(incl, part, preferred_element_type=F32) for part in _split3(logf))

    att = jnp.zeros((c, c), F32)
    sub_pos = row & (B_SUB - 1)
    for delta in range(B_SUB):
        if delta == 0:
            term = q * k
        else:
            k_s = pltpu.roll(k, delta, axis=0)
            b_s = pltpu.roll(b, delta, axis=0)
            term = (q * k_s) * jnp.exp(jnp.minimum(b - b_s, 0.0))
        diag = jnp.sum(term, axis=-1, keepdims=True)
        att = jnp.where((lane == row - delta) & (sub_pos >= delta), diag, att)

    rows = [jnp.zeros((B_SUB, c), F32)]
    for i in range(1, c // B_SUB):
        b_i = b[i * B_SUB - 1:i * B_SUB, :]
        kt = (k * jnp.exp(jnp.minimum(b_i - b, 0.0))).astype(BF16)
        qt = (q[i * B_SUB:(i + 1) * B_SUB] * jnp.exp(b[i * B_SUB:(i + 1) * B_SUB] - b_i)).astype(BF16)
        rows.append(lax.dot_general(qt, kt, (((1,), (1,)), ((), ())), preferred_element_type=F32))
    cross = jnp.concatenate(rows, axis=0)
    att = jnp.where((lane // B_SUB) < (row // B_SUB), cross, att)

    vb = v.astype(BF16)
    o = jnp.dot(att.astype(BF16), vb, preferred_element_type=F32)
    o = o + lax.dot_general((q * jnp.exp(b)).astype(BF16), state_t.astype(BF16), (((1,), (1,)), ((), ())),
                            preferred_element_type=F32)
    b_last = b[c - 1:c, :]
    kd = (k * jnp.exp(b_last - b)).astype(BF16)
    new_state_t = state_t * jnp.exp(b_last) + lax.dot_general(vb, kd, (((0,), (0,)), ((), ())),
                                                              preferred_element_type=F32)
    return o, new_state_t


def _hgrn_kernel(q_ref, f_ref, v_ref, g_ref, lbl_ref, gain_ref, o_ref, state_ref, *, layer, ts):
    @pl.when(pl.program_id(1) == 0)
    def _():
        state_ref[...] = jnp.zeros(state_ref.shape, F32)

    lbl = lbl_ref[...]
    e = jnp.exp(lbl - jnp.max(lbl, axis=0, keepdims=True))
    lb_all = jnp.sum(e[:layer + 1], axis=0, keepdims=True) / jnp.sum(e, axis=0, keepdims=True)
    gain = gain_ref[...]

    def chunk_body(ci, carry):
        r0 = pl.multiple_of(ci * B_CHUNK, B_CHUNK)
        for h in range(B_HEADS):
            cs = slice(h * B_HEAD_DIM, (h + 1) * B_HEAD_DIM)
            o, st = _hgrn_chunk(q_ref[0, pl.ds(r0, B_CHUNK), cs], f_ref[0, pl.ds(r0, B_CHUNK), cs],
                                v_ref[0, pl.ds(r0, B_CHUNK), cs], lb_all[:, cs], state_ref[h])
            state_ref[h] = st
            ms = jnp.mean(o * o, axis=-1, keepdims=True)
            o = (o * lax.rsqrt(ms + RMS_EPS)) * gain
            gate = g_ref[0, pl.ds(r0, B_CHUNK), cs]
            o_ref[0, pl.ds(r0, B_CHUNK), cs] = (o * (gate * jax.nn.sigmoid(gate))).astype(o_ref.dtype)
        return carry

    lax.fori_loop(0, ts // B_CHUNK, chunk_body, 0)


def hgrn2_mixer(zb, lb_logits, out_gain, layer, *, ts=256):
    b, s, _ = zb.shape
    width = B_HEADS * B_HEAD_DIM
    ts = min(ts, s)
    spec = lambda c: pl.BlockSpec((1, ts, width), lambda i, t, c=c: (i, t, c))
    return pl.pallas_call(
        functools.partial(_hgrn_kernel, layer=layer, ts=ts),
        grid=(b, s // ts),
        in_specs=[spec(0), spec(1), spec(2), spec(3),
                  pl.BlockSpec(lb_logits.shape, lambda i, t: (0, 0)),
                  pl.BlockSpec((1, B_HEAD_DIM), lambda i, t: (0, 0))],
        out_specs=pl.BlockSpec((1, ts, width), lambda i, t: (i, t, 0)),
        out_shape=jax.ShapeDtypeStruct((b, s, width), BF16),
        scratch_shapes=[pltpu.VMEM((B_HEADS, B_HEAD_DIM, B_HEAD_DIM), F32)],
        compiler_params=_cparams(("arbitrary", "arbitrary")),
        name="hgrn2_mixer",
    )(zb, zb, zb, zb, lb_logits.astype(F32), out_gain.reshape(1, B_HEAD_DIM).astype(F32))


def _shift_rows(x, prev8, k, row8):
    if k == 0:
        return x
    rolled = pltpu.roll(x, k, axis=0)
    head = jnp.where(row8 < k, pltpu.roll(prev8, k, axis=0), rolled[:8])
    return jnp.concatenate([head, rolled[8:]], axis=0)


def _rglru_kernel(g_ref, x_ref, cw_ref, cb_ref, wr_ref, br_ref, wi_ref, bi_ref, lam_ref, o_ref,
                  xprev_ref, hprev_ref, *, ts):
    @pl.when(pl.program_id(1) == 0)
    def _():
        xprev_ref[...] = jnp.zeros(xprev_ref.shape, F32)
        hprev_ref[...] = jnp.zeros(hprev_ref.shape, F32)

    x = x_ref[0]
    width = x.shape[1]
    prev8 = xprev_ref[...]
    row8 = lax.broadcasted_iota(jnp.int32, (8, width), 0)
    cw = cw_ref[...]
    xc = cb_ref[...] + sum(_shift_rows(x, prev8, k, row8) * cw[CONV_WIDTH - 1 - k:CONV_WIDTH - k, :]
                           for k in range(CONV_WIDTH))
    xprev_ref[...] = x[ts - 8:]

    xb = xc.astype(BF16)
    r = jax.nn.sigmoid(jnp.dot(xb, wr_ref[...], preferred_element_type=F32) + br_ref[...])
    ig = jax.nn.sigmoid(jnp.dot(xb, wi_ref[...], preferred_element_type=F32) + bi_ref[...])
    lam = lam_ref[...]
    softplus_neg = jnp.maximum(-lam, 0.0) + jnp.log(1.0 + jnp.exp(-jnp.abs(lam)))
    a = jnp.exp((-RG_C * r) * softplus_neg)
    u = jnp.sqrt(1.0 - a * a) * (ig * xc)

    row = lax.broadcasted_iota(jnp.int32, (ts, width), 0)
    d = 1
    while d < ts:
        keep = row >= d
        u = jnp.where(keep, a * pltpu.roll(u, d, axis=0) + u, u)
        a = jnp.where(keep, a * pltpu.roll(a, d, axis=0), a)
        d *= 2
    h = u + a * hprev_ref[0:1, :]
    hprev_ref[...] = jnp.broadcast_to(h[ts - 1:ts, :], hprev_ref.shape)

    gt = g_ref[0]
    gelu = 0.5 * gt * (1.0 + jnp.tanh(np.float32(math.sqrt(2.0 / math.pi)) * (gt + 0.044715 * (gt * gt * gt))))
    o_ref[0] = (gelu * h).astype(o_ref.dtype)


def rglru_mixer(zn, conv_w, conv_b, wr_bd, br, wi_bd, bi, lam, *, ts=256):
    b, s, two_c = zn.shape
    c = two_c // 2
    ts = min(ts, s)
    vec = lambda a: a.reshape(1, c).astype(F32)
    const = lambda shape: pl.BlockSpec(shape, lambda i, t: (0, 0))
    return pl.pallas_call(
        functools.partial(_rglru_kernel, ts=ts),
        grid=(b, s // ts),
        in_specs=[pl.BlockSpec((1, ts, c), lambda i, t: (i, t, 0)),
                  pl.BlockSpec((1, ts, c), lambda i, t: (i, t, 1)),
                  const((CONV_WIDTH, c)), const((1, c)), const((c, c)), const((1, c)),
                  const((c, c)), const((1, c)), const((1, c))],
        out_specs=pl.BlockSpec((1, ts, c), lambda i, t: (i, t, 0)),
        out_shape=jax.ShapeDtypeStruct((b, s, c), BF16),
        scratch_shapes=[pltpu.VMEM((8, c), F32), pltpu.VMEM((8, c), F32)],
        compiler_params=_cparams(("arbitrary", "arbitrary")),
        name="rglru_mixer",
    )(zn, zn, conv_w.astype(F32), vec(conv_b), wr_bd, vec(br), wi_bd, vec(bi), vec(lam))


def _block_diag(w):
    n, c, d = w.shape
    eye = jnp.eye(n, dtype=w.dtype)
    return (eye[:, None, :, None] * w[:, :, None, :]).reshape(n * c, n * d)


def _swa_kernel(qg_ref, kp_ref, kc_ref, vp_ref, vc_ref, sink_ref, o_ref):
    j = pl.program_id(1)
    row = lax.broadcasted_iota(jnp.int32, (BLK, BLK), 0)
    col = lax.broadcasted_iota(jnp.int32, (BLK, BLK), 1)
    bias_c = jnp.where(row <= col, 0.0, NEG_BIG)
    bias_p = jnp.where(row > col, 0.0, NEG_BIG)
    bias_p = jnp.where(j > 0, bias_p, NEG_BIG)
    bias_c = jnp.concatenate([bias_c] * GROUP, axis=1)
    bias_p = jnp.concatenate([bias_p] * GROUP, axis=1)
    kp = kp_ref[0]
    kc = kc_ref[0]
    outs = []
    for g in range(KV_HEADS):
        q = qg_ref[0, 0, :, g * GROUP * BLK:(g + 1) * GROUP * BLK]
        lg_p =jnp.dot(kp, q, preferred_element_type=F32) + bias_p
        lg_c = jnp.dot(kc, q, preferred_element_type=F32) + bias_c
        sink = sink_ref[g]
        m = jnp.maximum(jnp.maximum(jnp.max(lg_p, axis=0, keepdims=True), jnp.max(lg_c, axis=0, keepdims=True)), sink)
        p_p = jnp.exp(lg_p - m)
        p_c = jnp.exp(lg_c - m)
        den = jnp.sum(p_p, axis=0, keepdims=True) + jnp.sum(p_c, axis=0, keepdims=True) + jnp.exp(sink - m)
        rows = slice(g * HEAD_DIM, (g + 1) * HEAD_DIM)
        o = (jnp.dot(vp_ref[0, rows, :], p_p.astype(BF16), preferred_element_type=F32)
             + jnp.dot(vc_ref[0, rows, :], p_c.astype(BF16), preferred_element_type=F32))
        outs.append(o / den)
    for pair in range(Q_HEADS // 2):
        g, hh = divmod(2 * pair, GROUP)
        o = outs[g]
        two = jnp.concatenate([o[:, hh * BLK:(hh + 1) * BLK], o[:, (hh + 1) * BLK:(hh + 2) * BLK]], axis=0)
        o_ref[0, :, pair * BLK:(pair + 1) * BLK] = two.T.astype(o_ref.dtype)


def swa_attention(qg, k, vt, sinks):
    b, s, _ = k.shape
    nqb = s // BLK
    sink_rows = jnp.repeat(sinks.astype(F32), BLK).reshape(KV_HEADS, 1, GROUP * BLK)
    prev = lambda j: jnp.maximum(j - 1, 0)
    return pl.pallas_call(
        _swa_kernel,
        grid=(b, nqb),
        in_specs=[
            pl.BlockSpec((1, 1, BLK, Q_HEADS * BLK), lambda i, j: (i, j, 0, 0)),
            pl.BlockSpec((1, BLK, KV_HEADS * HEAD_DIM), lambda i, j: (i, prev(j), 0)),
            pl.BlockSpec((1, BLK, KV_HEADS * HEAD_DIM), lambda i, j: (i, j, 0)),
            pl.BlockSpec((1, KV_HEADS * HEAD_DIM, BLK), lambda i, j: (i, 0, prev(j))),
            pl.BlockSpec((1, KV_HEADS * HEAD_DIM, BLK), lambda i, j: (i, 0, j)),
            pl.BlockSpec((KV_HEADS, 1, GROUP * BLK), lambda i, j: (0, 0, 0)),
        ],
        out_specs=pl.BlockSpec((1, BLK, Q_HEADS * HEAD_DIM), lambda i, j: (i, j, 0)),
        out_shape=jax.ShapeDtypeStruct((b, s, Q_HEADS * HEAD_DIM), BF16),
        compiler_params=_cparams(("arbitrary", "arbitrary")),
        name="swa_attention",
    )(qg, k, k, vt, vt, sink_rows)


def _mlp_kernel(x_ref, oa_ref, ob_ref, wo_ref, g_ref, wu_ref, wd_ref, out_ref, *, hc):
    half = oa_ref.shape[1]
    x1 = (x_ref[...] + jnp.dot(oa_ref[...], wo_ref[:half, :], preferred_element_type=F32)
          + jnp.dot(ob_ref[...], wo_ref[half:, :], preferred_element_type=F32))
    ms = jnp.mean(x1 * x1, axis=-1, keepdims=True)
    hn = ((x1 * lax.rsqrt(ms + RMS_EPS)) * g_ref[...]).astype(BF16)
    out_ref[...] = x1
    hidden = wu_ref.shape[1]
    for c in range(hidden // hc):
        up = jnp.dot(hn, wu_ref[:, c * hc:(c + 1) * hc], preferred_element_type=F32)
        act = jnp.square(jnp.maximum(up, 0.0)).astype(BF16)
        out_ref[...] += jnp.dot(act, wd_ref[c * hc:(c + 1) * hc, :], preferred_element_type=F32)


def out_proj_mlp(x, oa, ob, w_out, gain, w_up, w_down, *, tm=512, hc=1024):
    b, s, d = x.shape
    t = b * s
    tm = min(tm, t)
    half = oa.shape[-1]
    hidden = w_up.shape[1]
    const = lambda shape: pl.BlockSpec(shape, lambda i: (0, 0), pipeline_mode=pl.Buffered(1))
    out = pl.pallas_call(
        functools.partial(_mlp_kernel, hc=hc),
        grid=(t // tm,),
        in_specs=[pl.BlockSpec((tm, d), lambda i: (i, 0)),
                  pl.BlockSpec((tm, half), lambda i: (i, 0)),
                  pl.BlockSpec((tm, half), lambda i: (i, 0)),
                  const((d, d)), const((1, d)), const((d, hidden)), const((hidden, d))],
        out_specs=pl.BlockSpec((tm, d), lambda i: (i, 0)),
        out_shape=jax.ShapeDtypeStruct((t, d), F32),
        compiler_params=_cparams(("arbitrary",)),
        name="out_proj_mlp",
    )(x.reshape(t, d), oa.reshape(t, half), ob.reshape(t, half), w_out.astype(BF16),
      gain.reshape(1, d).astype(F32), w_up.astype(BF16), w_down.astype(BF16))
    return out.reshape(b, s, d)


N_EVEN_T = Q_HEADS * HEAD_DIM + 2 * KV_HEADS * HEAD_DIM + IDX_HEADS * HEAD_DIM + HEAD_DIM + IDX_HEADS
N_ODD_NAT = 1024


def layer0_attention(x, pos, inp, j, rope=None):
    cos_t, sin_t = rope if rope is not None else rope_tables_t(pos)
    w_in = inp['even_w_in'][j]
    w_t = w_in[:, :N_EVEN_T].T.astype(BF16)
    w_n = w_in[:, N_EVEN_T:].astype(BF16)
    qg, k, vt, iq, ik, iw, zb = input_projection(
        x, inp['norm_mix_g'][2 * j], w_t, w_n, cos_t, sin_t, inp['a_q_norm_g'][j], inp['a_k_norm_g'][j],
        with_indexer=True)
    return dsa_attention(qg, k, vt, iq, ik, iw), zb


def layer1_mixers(x, pos, inp, j, rope=None):
    cos_t, sin_t = rope if rope is not None else rope_tables_t(pos)
    w_in = inp['odd_w_in'][j]
    w_n = w_in[:, :N_ODD_NAT].astype(BF16)
    w_t = w_in[:, N_ODD_NAT:].T.astype(BF16)
    qg, k, vt, zn = input_projection(
        x, inp['norm_mix_g'][2 * j + 1], w_t, w_n, cos_t, sin_t, inp['d_q_norm_g'][j], inp['d_k_norm_g'][j],
        with_indexer=False)
    o_c = rglru_mixer(zn, inp['c_conv_w'][j], inp['c_conv_b'][j],
                      _block_diag(inp['c_rgate_w'][j]).astype(BF16), inp['c_rgate_b'][j],
                      _block_diag(inp['c_igate_w'][j]).astype(BF16), inp['c_igate_b'][j], inp['c_lambda'][j])
    o_d = swa_attention(qg, k, vt, inp['d_sinks'][j])
    return o_c, o_d


def kernel(x, positions, norm_mix_g, norm_mlp_g, even_w_in, even_w_out, a_q_norm_g, a_k_norm_g, b_lb_logits,
           b_out_norm_g, odd_w_in, odd_w_out, c_conv_w, c_conv_b, c_rgate_w, c_rgate_b, c_igate_w, c_igate_b,
           c_lambda, d_q_norm_g, d_k_norm_g, d_sinks, mlp_w_up, mlp_w_down):
    inp = dict(norm_mix_g=norm_mix_g, even_w_in=even_w_in, a_q_norm_g=a_q_norm_g, a_k_norm_g=a_k_norm_g,
               odd_w_in=odd_w_in, c_conv_w=c_conv_w, c_conv_b=c_conv_b, c_rgate_w=c_rgate_w, c_rgate_b=c_rgate_b,
               c_igate_w=c_igate_w, c_igate_b=c_igate_b, c_lambda=c_lambda, d_q_norm_g=d_q_norm_g,
               d_k_norm_g=d_k_norm_g, d_sinks=d_sinks)
    depth = norm_mix_g.shape[0]
    rope = rope_tables_t(positions)
    for layer in range(depth):
        j = layer // 2
        if layer % 2 == 0:
            o_a, zb = layer0_attention(x, positions, inp, j, rope)
            o_b = hgrn2_mixer(zb, b_lb_logits, b_out_norm_g[j], j)
            x = out_proj_mlp(x, o_a, o_b, even_w_out[j], norm_mlp_g[layer], mlp_w_up[layer], mlp_w_down[layer])
        else:
            o_c, o_d = layer1_mixers(x, positions, inp, j, rope)
            x = out_proj_mlp(x, o_c, o_d, odd_w_out[j], norm_mlp_g[layer], mlp_w_up[layer], mlp_w_down[layer])
    return x
```

```python
import functools
import math

import numpy as np
import jax
import jax.numpy as jnp
from jax import lax
from jax.experimental import pallas as pl
from jax.experimental.pallas import tpu as pltpu

F32 = jnp.float32
BF16 = jnp.bfloat16

HEAD_DIM = 64
ROPE_THETA = 10000.0
RMS_EPS = 1e-6
KV_HEADS = 2
Q_HEADS = 8
GROUP = Q_HEADS // KV_HEADS
IDX_HEADS = 8
DSA_TOPK = 256
BLK = 128
B_HEAD_DIM = 128
B_HEADS = 4
B_CHUNK = 64
B_SUB = 16
CONV_WIDTH = 4
RG_C = 8.0
C_BLOCKS = 8
NEG_BIG = -1e30
LOG2E = math.log2(math.e)
INT_MIN = -2 ** 31

VMEM_LIMIT = 56 * 1024 * 1024


def _cparams(sem):
    return pltpu.CompilerParams(dimension_semantics=sem, vmem_limit_bytes=VMEM_LIMIT)


def _rope_kernel(pos_ref, invf_ref, cos_ref, sin_ref):
    ang = pos_ref[0].astype(F32) * invf_ref[...]
    cos_ref[0] = jnp.cos(ang)
    sin_ref[0] = jnp.sin(ang)


def rope_tables_t(positions):
    b, s = positions.shape
    half = HEAD_DIM // 2
    invf = (1.0 / (ROPE_THETA ** (jnp.arange(0, HEAD_DIM, 2, dtype=F32) / HEAD_DIM))).reshape(half, 1)
    out = jax.ShapeDtypeStruct((b, half, s), F32)
    return pl.pallas_call(
        _rope_kernel,
        grid=(b,),
        in_specs=[pl.BlockSpec((1, 1, s), lambda i: (i, 0, 0)),
                  pl.BlockSpec((half, 1), lambda i: (0, 0))],
        out_specs=[pl.BlockSpec((1, half, s), lambda i: (i, 0, 0))] * 2,
        out_shape=[out, out],
        compiler_params=_cparams(("arbitrary",)),
        name="rope_tables",
    )(positions.reshape(b, 1, s), invf)


def _head_norm_rope_t(blk, gain_col, cos, sin, scale):
    if gain_col is not None:
        ms = jnp.mean(blk * blk, axis=0, keepdims=True)
        blk = (blk * lax.rsqrt(ms + RMS_EPS)) * gain_col
    half = HEAD_DIM // 2
    x1, x2 = blk[:half], blk[half:]
    o1 = x1 * cos - x2 * sin
    o2 = x2 * cos + x1 * sin
    out = jnp.concatenate([o1, o2], axis=0)
    if scale != 1.0:
        out = out * scale
    return out


def _proj_kernel(*refs, tm, with_indexer, n_nat):
    if with_indexer:
        (x_ref, g_ref, wt_ref, wn_ref, cos_ref, sin_ref, qgain_ref, kgain_ref,
         qg_ref, k_ref, vt_ref, iq_ref, ik_ref, iw_ref, nat_ref) = refs
    else:
        (x_ref, g_ref, wt_ref, wn_ref, cos_ref, sin_ref, qgain_ref, kgain_ref,
         qg_ref, k_ref, vt_ref, nat_ref) = refs
    nq = tm // BLK
    x = x_ref[0]
    ms = jnp.mean(x * x, axis=-1, keepdims=True)
    hn = ((x * lax.rsqrt(ms + RMS_EPS)) * g_ref[...]).astype(BF16)
    zt = lax.dot_general(wt_ref[...], hn, (((1,), (1,)), ((), ())), preferred_element_type=F32)
    nat_ref[0] = jnp.dot(hn, wn_ref[...], preferred_element_type=F32)
    cos = cos_ref[0]
    sin = sin_ref[0]
    qgain = qgain_ref[...]
    kgain = kgain_ref[...]
    zeros64 = jnp.zeros((HEAD_DIM, BLK), BF16)
    r = 0
    for h in range(Q_HEADS):
        g, hh = divmod(h, GROUP)
        q = _head_norm_rope_t(zt[r:r + HEAD_DIM], qgain, cos, sin, HEAD_DIM ** -0.5 * LOG2E).astype(BF16)
        r += HEAD_DIM
        for j in range(nq):
            qg_ref[0, j, g * HEAD_DIM:(g + 1) * HEAD_DIM, h * BLK:(h + 1) * BLK] = q[:, j * BLK:(j + 1) * BLK]
            qg_ref[0, j, (1 - g) * HEAD_DIM:(2 - g) * HEAD_DIM, h * BLK:(h + 1) * BLK] = zeros64
    ks = []
    for g in range(KV_HEADS):
        ks.append(_head_norm_rope_t(zt[r:r + HEAD_DIM], kgain, cos, sin, 1.0))
        r += HEAD_DIM
    k_ref[0] = jnp.concatenate(ks, axis=0).T.astype(BF16)
    vt_ref[0] = zt[r:r + KV_HEADS * HEAD_DIM].astype(BF16)
    r += KV_HEADS * HEAD_DIM
    if with_indexer:
        for h in range(IDX_HEADS):
            iq = _head_norm_rope_t(zt[r:r + HEAD_DIM], None, cos, sin, 1.0).astype(BF16)
            r += HEAD_DIM
            for j in range(nq):
                iq_ref[0, j, 0:HEAD_DIM, h * BLK:(h + 1) * BLK] = iq[:, j * BLK:(j + 1) * BLK]
                iq_ref[0, j, HEAD_DIM:2 * HEAD_DIM, h * BLK:(h + 1) * BLK] = zeros64
        ikt = _head_norm_rope_t(zt[r:r + HEAD_DIM], None, cos, sin, 1.0)
        r += HEAD_DIM
        ik_ref[0] = jnp.concatenate([ikt, jnp.zeros_like(ikt)], axis=0).T.astype(BF16)
        iw_ref[0] = zt[r:r + IDX_HEADS] * (HEAD_DIM ** -0.5 * IDX_HEADS ** -0.5)
        r += IDX_HEADS


def input_projection(x, gain, w_t, w_n, cos_t, sin_t, q_gain, k_gain, *, with_indexer, tm=512):
    b, s, d = x.shape
    tm = min(tm, s)
    nt = w_t.shape[0]
    n_nat = w_n.shape[1]
    nqb = s // BLK
    half = HEAD_DIM // 2
    grid = (b, s // tm)
    in_specs = [
        pl.BlockSpec((1, tm, d), lambda i, j: (i, j, 0)),
        pl.BlockSpec((1, d), lambda i, j: (0, 0)),
        pl.BlockSpec((nt, d), lambda i, j: (0, 0)),
        pl.BlockSpec((d, n_nat), lambda i, j: (0, 0)),
        pl.BlockSpec((1, half, tm), lambda i, j: (i, 0, j)),
        pl.BlockSpec((1, half, tm), lambda i, j: (i, 0, j)),
        pl.BlockSpec((HEAD_DIM, 1), lambda i, j: (0, 0)),
        pl.BlockSpec((HEAD_DIM, 1), lambda i, j: (0, 0)),
    ]
    out_shape = [
        jax.ShapeDtypeStruct((b, nqb, BLK, Q_HEADS * BLK), BF16),
        jax.ShapeDtypeStruct((b, s, KV_HEADS * HEAD_DIM), BF16),
        jax.ShapeDtypeStruct((b, KV_HEADS * HEAD_DIM, s), BF16),
    ]
    out_specs = [
        pl.BlockSpec((1, tm // BLK, BLK, Q_HEADS * BLK), lambda i, j: (i, j, 0, 0)),
        pl.BlockSpec((1, tm, KV_HEADS * HEAD_DIM), lambda i, j: (i, j, 0)),
        pl.BlockSpec((1, KV_HEADS * HEAD_DIM, tm), lambda i, j: (i, 0, j)),
    ]
    if with_indexer:
        out_shape += [
            jax.ShapeDtypeStruct((b, nqb, BLK, IDX_HEADS * BLK), BF16),
            jax.ShapeDtypeStruct((b, s, BLK), BF16),
            jax.ShapeDtypeStruct((b, IDX_HEADS, s), F32),
        ]
        out_specs += [
            pl.BlockSpec((1, tm // BLK, BLK, IDX_HEADS * BLK), lambda i, j: (i, j, 0, 0)),
            pl.BlockSpec((1, tm, BLK), lambda i, j: (i, j, 0)),
            pl.BlockSpec((1, IDX_HEADS, tm), lambda i, j: (i, 0, j)),
        ]
    out_shape.append(jax.ShapeDtypeStruct((b, s, n_nat), F32))
    out_specs.append(pl.BlockSpec((1, tm, n_nat), lambda i, j: (i, j, 0)))
    return pl.pallas_call(
        functools.partial(_proj_kernel, tm=tm, with_indexer=with_indexer, n_nat=n_nat),
        grid=grid,
        in_specs=in_specs,
        out_specs=out_specs,
        out_shape=out_shape,
        compiler_params=_cparams(("arbitrary", "arbitrary")),
        name="in_proj_dsa" if with_indexer else "in_proj_swa",
    )(x, gain.reshape(1, d), w_t, w_n, cos_t, sin_t, q_gain.reshape(HEAD_DIM, 1), k_gain.reshape(HEAD_DIM, 1))


PAIR = 2 * BLK
QUAD = 4 * BLK
COUNT_ROWS = 64
FLT_LOWEST = float(np.finfo(np.float32).min)
SEARCH_FIRST_STEPS = 12
SEARCH_STEPS_PER_CHECK = 4
SEARCH_MAX_STEPS = 64


def _fold8(x):
    return x.reshape(x.shape[0] // 8, 8, x.shape[1])


def _pair_loop(npair, pair_fn, carry):
    def two(i, c):
        return pair_fn(2 * i + 1, pair_fn(2 * i, c))
    carry = lax.fori_loop(0, npair // 2, two, carry)
    return lax.fori_loop(2 * (npair // 2), npair, pair_fn, carry)


def _count(sc_ref, nquad, thr):
    def body(i, acc):
        off = pl.multiple_of(i * QUAD, QUAD)
        hit = jnp.where(sc_ref[pl.ds(off, QUAD), :] >= thr, 1.0, 0.0)
        return acc + jnp.sum(hit.reshape(QUAD // COUNT_ROWS, COUNT_ROWS, BLK), axis=0)
    acc = lax.fori_loop(0, nquad, body, jnp.zeros((COUNT_ROWS, BLK), F32))
    return jnp.sum(acc, axis=0, keepdims=True)


def _drop_excess(sc_ref, nquad, thr, excess, before):
    def cond(ex):
        return jnp.max(ex) > 0.0

    def body(ex):
        def min_body(i, acc):
            off = pl.multiple_of(i * QUAD, QUAD)
            blk = sc_ref[pl.ds(off, QUAD), :]
            return jnp.minimum(acc, jnp.min(_fold8(jnp.where(blk >= thr, blk, jnp.inf)), axis=0))
        low = jnp.min(lax.fori_loop(0, nquad, min_body, jnp.full((8, BLK), jnp.inf, F32)), axis=0, keepdims=True)

        def cnt_body(i, acc):
            off = pl.multiple_of(i * QUAD, QUAD)
            return acc + jnp.sum(_fold8(jnp.where(sc_ref[pl.ds(off, QUAD), :] == low, 1.0, 0.0)), axis=0)
        copies = jnp.sum(lax.fori_loop(0, nquad, cnt_body, jnp.zeros((8, BLK), F32)), axis=0, keepdims=True)
        drop = jnp.where(ex > 0.0, jnp.minimum(copies, ex), 0.0)
        keep = copies - drop

        def drop_body(kb, seen):
            off = pl.multiple_of(kb * BLK, BLK)
            blk = sc_ref[pl.ds(off, BLK), :]
            eq = blk == low
            eqf = jnp.where(eq, 1.0, 0.0)
            rank = jnp.dot(before, eqf.astype(BF16), preferred_element_type=F32) + seen
            sc_ref[pl.ds(off, BLK), :] = jnp.where(eq & (rank >= keep), -jnp.inf, blk)
            return seen + jnp.sum(eqf, axis=0, keepdims=True)
        lax.fori_loop(0, 4 * nquad, drop_body, jnp.zeros((1, BLK), F32))
        return ex - drop

    lax.while_loop(cond, body, excess)


def _search_step(sc_ref, nquad, topk, st):
    lo, hi, c_lo, c_hi, done = st
    guess = lo + (hi - lo) * 0.5
    inside = (guess > lo) & (guess < hi)
    c = _count(sc_ref, nquad, guess)
    move = inside & (done == 0.0)
    up = move & (c >= topk)
    down = move & (c < topk)
    lo = jnp.where(up, guess, lo)
    c_lo = jnp.where(up, c, c_lo)
    hi = jnp.where(down, guess, hi)
    c_hi = jnp.where(down, c, c_hi)
    done = jnp.where(inside & (c_lo != topk), done, 1.0)
    return lo, hi, c_lo, c_hi, done


def _dsa_kernel(q_ref, k_ref, vt_ref, iq_ref, ik_ref, iw_ref, o_ref, sc_ref, lg_ref, acc_ref, *, topk):
    qi = pl.program_id(1)
    npair = (qi + 2) // 2
    iq = iq_ref[0, 0]
    w = iw_ref[0]
    nquad = (qi + 4) // 4
    row2 = lax.broadcasted_iota(jnp.int32, (PAIR, BLK), 0)
    col2 = lax.broadcasted_iota(jnp.int32, (PAIR, BLK), 1)

    def score_body(pi, carry):
        lo8, hi8 = carry
        off = pl.multiple_of(pi * PAIR, PAIR)
        ikb = ik_ref[0, pl.ds(off, PAIR), :]
        tot = jnp.zeros((PAIR, BLK), F32)
        for h in range(IDX_HEADS):
            sc = jnp.dot(ikb, iq_ref[0, 0, :, h * BLK:(h + 1) * BLK], preferred_element_type=F32)
            tot = tot + jnp.maximum(sc, 0.0) * w[h:h + 1, :]
        causal = (pi * PAIR + row2) <= (qi * BLK + col2)
        sc_ref[pl.ds(off, PAIR), :] = jnp.where(causal, tot, -jnp.inf)
        lo8 = jnp.minimum(lo8, jnp.min(_fold8(jnp.where(causal, tot, jnp.inf)), axis=0))
        hi8 = jnp.maximum(hi8, jnp.max(_fold8(jnp.where(causal, tot, -jnp.inf)), axis=0))
        return lo8, hi8

    lo8, hi8 = _pair_loop(npair, score_body,
                          (jnp.full((8, BLK), jnp.inf, F32), jnp.full((8, BLK), -jnp.inf, F32)))

    @pl.when(2 * npair < 4 * nquad)
    def _():
        sc_ref[pl.ds(pl.multiple_of(npair * PAIR, PAIR), PAIR), :] = jnp.full((PAIR, BLK), -jnp.inf, F32)

    smin = jnp.min(lo8, axis=0, keepdims=True)
    smax = jnp.max(hi8, axis=0, keepdims=True)
    n_valid = (qi * BLK + 1 + lax.broadcasted_iota(jnp.int32, (1, BLK), 1)).astype(F32)
    few = n_valid <= topk
    above = smax + (jnp.abs(smax) * 1e-6 + 1e-30)
    state = (smin, above, n_valid, jnp.zeros((1, BLK), F32), jnp.where(few, 1.0, 0.0))

    def search_cond(st):
        i, inner = st
        return (i < SEARCH_MAX_STEPS) & (jnp.min(inner[4]) == 0.0)

    def search_body(st):
        i, inner = st
        for _ in range(SEARCH_STEPS_PER_CHECK):
            inner = _search_step(sc_ref, nquad, topk, inner)
        return i + SEARCH_STEPS_PER_CHECK, inner

    _, state = lax.fori_loop(0, SEARCH_FIRST_STEPS // SEARCH_STEPS_PER_CHECK, lambda i, st: search_body(st),
                             (jnp.int32(0), state))
    _, (lo, _, c_lo, _, _) = lax.while_loop(search_cond, search_body, (jnp.int32(SEARCH_FIRST_STEPS), state))
    thr = jnp.where(few, FLT_LOWEST, lo)
    excess = jnp.where(few, 0.0, c_lo - topk)

    @pl.when(jnp.max(excess) > 0.0)
    def _():
        row = lax.broadcasted_iota(jnp.int32, (BLK, BLK), 0)
        col = lax.broadcasted_iota(jnp.int32, (BLK, BLK), 1)
        _drop_excess(sc_ref, nquad, thr, excess, (col < row).astype(BF16))

    q_all = q_ref[0, 0]

    def logit_body(pi, mx):
        off = pl.multiple_of(pi * PAIR, PAIR)
        kb = k_ref[0, pl.ds(off, PAIR), :]
        bias = jnp.where(sc_ref[pl.ds(off, PAIR), :] >= thr, 0.0, NEG_BIG)
        parts = []
        for h in range(Q_HEADS):
            lgh = jnp.dot(kb, q_ref[0, 0, :, h * BLK:(h + 1) * BLK], preferred_element_type=F32) + bias
            lg_ref[pl.ds(off, PAIR), h * BLK:(h + 1) * BLK] = lgh
            parts.append(jnp.max(_fold8(lgh), axis=0))
        return jnp.maximum(mx, jnp.concatenate(parts, axis=1))

    mx = _pair_loop(npair, logit_body, jnp.full((8, Q_HEADS * BLK), NEG_BIG, F32))
    m = jnp.max(mx, axis=0, keepdims=True)

    acc_ref[...] = jnp.zeros(acc_ref.shape, F32)

    def pv_body(pi, ls):
        off = pl.multiple_of(pi * PAIR, PAIR)
        sums, pbs = [], []
        for h in range(Q_HEADS):
            cs = slice(h * BLK, (h + 1) * BLK)
            p = jnp.exp2(lg_ref[pl.ds(off, PAIR), cs] - m[:, cs])
            sums.append(jnp.sum(_fold8(p), axis=0))
            pbs.append(p.astype(BF16))
        for g in range(KV_HEADS):
            pb = jnp.concatenate(pbs[g * GROUP:(g + 1) * GROUP], axis=1)
            vt = vt_ref[0, g * HEAD_DIM:(g + 1) * HEAD_DIM, pl.ds(off, PAIR)]
            acc_ref[g] += jnp.dot(vt, pb, preferred_element_type=F32)
        return ls + jnp.concatenate(sums, axis=1)

    ls = _pair_loop(npair, pv_body, jnp.zeros((8, Q_HEADS * BLK), F32))
    l = jnp.sum(ls, axis=0, keepdims=True)

    for pair in range(Q_HEADS // 2):
        g, hh = divmod(2 * pair, GROUP)
        o = acc_ref[g] / l[:, g * GROUP * BLK:(g + 1) * GROUP * BLK]
        two = jnp.concatenate([o[:, hh * BLK:(hh + 1) * BLK], o[:, (hh + 1) * BLK:(hh + 2) * BLK]], axis=0)
        o_ref[0, :, pair * BLK:(pair + 1) * BLK] = two.T.astype(o_ref.dtype)


def dsa_attention(qg, k, vt, iq, ik, iw):
    b, s, _ = k.shape
    assert s % QUAD == 0, "the counting loops pad the score tile to whole groups of four key blocks"
    nqb = s // BLK
    topk = min(DSA_TOPK, s // 4)
    return pl.pallas_call(
        functools.partial(_dsa_kernel, topk=topk),
        grid=(b, nqb),
        in_specs=[
            pl.BlockSpec((1, 1, BLK, Q_HEADS * BLK), lambda i, j: (i, j, 0, 0)),
            pl.BlockSpec((1, s, KV_HEADS * HEAD_DIM), lambda i, j: (i, 0, 0)),
            pl.BlockSpec((1, KV_HEADS * HEAD_DIM, s), lambda i, j: (i, 0, 0)),
            pl.BlockSpec((1, 1, BLK, IDX_HEADS * BLK), lambda i, j: (i, j, 0, 0)),
            pl.BlockSpec((1, s, BLK), lambda i, j: (i, 0, 0)),
            pl.BlockSpec((1, IDX_HEADS, BLK), lambda i, j: (i, 0, j)),
        ],
        out_specs=pl.BlockSpec((1, BLK, Q_HEADS * HEAD_DIM), lambda i, j: (i, j, 0)),
        out_shape=jax.ShapeDtypeStruct((b, s, Q_HEADS * HEAD_DIM), BF16),
        scratch_shapes=[
            pltpu.VMEM((s, BLK), F32),
            pltpu.VMEM((s, Q_HEADS * BLK), F32),
            pltpu.VMEM((KV_HEADS, HEAD_DIM, GROUP * BLK), F32),
        ],
        compiler_params=_cparams(("arbitrary", "arbitrary")),
        name="dsa_attention",
    )(qg, k, vt, iq, ik, iw)


def _split3(x):
    hi = x.astype(BF16)
    r1 = x - hi.astype(F32)
    mid = r1.astype(BF16)
    lo = (r1 - mid.astype(F32)).astype(BF16)
    return hi, mid, lo


def _hgrn_chunk(qraw, flog, v, lb, state_t):
    c = B_CHUNK
    q = qraw * jax.nn.sigmoid(qraw)
    f = lb + (1.0 - lb) * jax.nn.sigmoid(flog)
    k = 1.0 - f
    logf = jnp.log(f)
    row = lax.broadcasted_iota(jnp.int32, (c, c), 0)
    lane = lax.broadcasted_iota(jnp.int32, (c, c), 1)
    incl = (lane <= row).astype(BF16)
    b = sum(jnp.dot(incl, part, preferred_element_type=F32) for part in _split3(logf))

    att = jnp.zeros((c, c), F32)
    sub_pos = row & (B_SUB - 1)
    kf = k
    for delta in range(B_SUB):
        if delta > 0:
            kf = pltpu.roll(kf, 1, axis=0) * f
        diag = jnp.sum(q * kf, axis=-1, keepdims=True)
        att = jnp.where((lane == row - delta) & (sub_pos >= delta), diag, att)

    rows = [jnp.zeros((B_SUB, c), F32)]
    for i in range(1, c // B_SUB):
        b_i = b[i * B_SUB - 1:i * B_SUB, :]
        kt = (k * jnp.exp(jnp.minimum(b_i - b, 0.0))).astype(BF16)
        qt = (q[i * B_SUB:(i + 1) * B_SUB] * jnp.exp(b[i * B_SUB:(i + 1) * B_SUB] - b_i)).astype(BF16)
        rows.append(lax.dot_general(qt, kt, (((1,), (1,)), ((), ())), preferred_element_type=F32))
    cross = jnp.concatenate(rows, axis=0)
    att = jnp.where((lane // B_SUB) < (row // B_SUB), cross, att)

    vb = v.astype(BF16)
    o = jnp.dot(att.astype(BF16), vb, preferred_element_type=F32)
    o = o + lax.dot_general((q * jnp.exp(b)).astype(BF16), state_t.astype(BF16), (((1,), (1,)), ((), ())),
                            preferred_element_type=F32)
    b_last = b[c - 1:c, :]
    kd = (k * jnp.exp(b_last - b)).astype(BF16)
    new_state_t = state_t * jnp.exp(b_last) + lax.dot_general(vb, kd, (((0,), (0,)), ((), ())),
                                                              preferred_element_type=F32)
    return o, new_state_t


def _hgrn_kernel(q_ref, f_ref, v_ref, g_ref, lbl_ref, gain_ref, o_ref, state_ref, *, layer, ts):
    @pl.when(pl.program_id(1) == 0)
    def _():
        state_ref[...] = jnp.zeros(state_ref.shape, F32)

    lbl = lbl_ref[...]
    e = jnp.exp(lbl - jnp.max(lbl, axis=0, keepdims=True))
    lb_all = jnp.sum(e[:layer + 1], axis=0, keepdims=True) / jnp.sum(e, axis=0, keepdims=True)
    gain = gain_ref[...]

    def chunk_body(ci, carry):
        r0 = pl.multiple_of(ci * B_CHUNK, B_CHUNK)
        for h in range(B_HEADS):
            cs = slice(h * B_HEAD_DIM, (h + 1) * B_HEAD_DIM)
            o, st = _hgrn_chunk(q_ref[0, pl.ds(r0, B_CHUNK), cs], f_ref[0, pl.ds(r0, B_CHUNK), cs],
                                v_ref[0, pl.ds(r0, B_CHUNK), cs], lb_all[:, cs], state_ref[h])
            state_ref[h] = st
            ms = jnp.mean(o * o, axis=-1, keepdims=True)
            o = (o * lax.rsqrt(ms + RMS_EPS)) * gain
            gate = g_ref[0, pl.ds(r0, B_CHUNK), cs]
            o_ref[0, pl.ds(r0, B_CHUNK), cs] = (o * (gate * jax.nn.sigmoid(gate))).astype(o_ref.dtype)
        return carry

    lax.fori_loop(0, ts // B_CHUNK, chunk_body, 0, unroll=2)


def hgrn2_mixer(zb, lb_logits, out_gain, layer, *, ts=256):
    b, s, _ = zb.shape
    width = B_HEADS * B_HEAD_DIM
    ts = min(ts, s)
    spec = lambda c: pl.BlockSpec((1, ts, width), lambda i, t, c=c: (i, t, c))
    return pl.pallas_call(
        functools.partial(_hgrn_kernel, layer=layer, ts=ts),
        grid=(b, s // ts),
        in_specs=[spec(0), spec(1), spec(2), spec(3),
                  pl.BlockSpec(lb_logits.shape, lambda i, t: (0, 0)),
                  pl.BlockSpec((1, B_HEAD_DIM), lambda i, t: (0, 0))],
        out_specs=pl.BlockSpec((1, ts, width), lambda i, t: (i, t, 0)),
        out_shape=jax.ShapeDtypeStruct((b, s, width), BF16),
        scratch_shapes=[pltpu.VMEM((B_HEADS, B_HEAD_DIM, B_HEAD_DIM), F32)],
        compiler_params=_cparams(("arbitrary", "arbitrary")),
        name="hgrn2_mixer",
    )(zb, zb, zb, zb, lb_logits.astype(F32), out_gain.reshape(1, B_HEAD_DIM).astype(F32))


def _shift_rows(x, prev8, k, row8):
    if k == 0:
        return x
    rolled = pltpu.roll(x, k, axis=0)
    head = jnp.where(row8 < k, pltpu.roll(prev8, k, axis=0), rolled[:8])
    return jnp.concatenate([head, rolled[8:]], axis=0)


def _rglru_kernel(g_ref, x_ref, cw_ref, cb_ref, wr_ref, br_ref, wi_ref, bi_ref, lam_ref, o_ref,
                  xprev_ref, hprev_ref, *, ts):
    @pl.when(pl.program_id(1) == 0)
    def _():
        xprev_ref[...] = jnp.zeros(xprev_ref.shape, F32)
        hprev_ref[...] = jnp.zeros(hprev_ref.shape, F32)

    x = x_ref[0]
    width = x.shape[1]
    prev8 = xprev_ref[...]
    row8 = lax.broadcasted_iota(jnp.int32, (8, width), 0)
    cw = cw_ref[...]
    xc = cb_ref[...] + sum(_shift_rows(x, prev8, k, row8) * cw[CONV_WIDTH - 1 - k:CONV_WIDTH - k, :]
                           for k in range(CONV_WIDTH))
    xprev_ref[...] = x[ts - 8:]

    xb = xc.astype(BF16)
    r = jax.nn.sigmoid(jnp.dot(xb, wr_ref[...], preferred_element_type=F32) + br_ref[...])
    ig = jax.nn.sigmoid(jnp.dot(xb, wi_ref[...], preferred_element_type=F32) + bi_ref[...])
    lam = lam_ref[...]
    softplus_neg = jnp.maximum(-lam, 0.0) + jnp.log(1.0 + jnp.exp(-jnp.abs(lam)))
    a = jnp.exp((-RG_C * r) * softplus_neg)
    u = jnp.sqrt(1.0 - a * a) * (ig * xc)

    pos = lax.broadcasted_iota(jnp.int32, (ts, width), 0) & 7
    d = 1
    while d < 8:
        keep = pos >= d
        u = jnp.where(keep, a * pltpu.roll(u, d, axis=0) + u, u)
        a = jnp.where(keep, a * pltpu.roll(a, d, axis=0), a)
        d *= 2
    carry = hprev_ref[0:1, :]
    hs = []
    for g in range(ts // 8):
        hg = u[8 * g:8 * g + 8] + a[8 * g:8 * g + 8] * carry
        carry = hg[7:8]
        hs.append(hg)
    h = jnp.concatenate(hs, axis=0)
    hprev_ref[...] = jnp.broadcast_to(carry, hprev_ref.shape)

    gt = g_ref[0]
    gelu = 0.5 * gt * (1.0 + jnp.tanh(np.float32(math.sqrt(2.0 / math.pi)) * (gt + 0.044715 * (gt * gt * gt))))
    o_ref[0] = (gelu * h).astype(o_ref.dtype)


def rglru_mixer(zn, conv_w, conv_b, wr_bd, br, wi_bd, bi, lam, *, ts=256):
    b, s, two_c = zn.shape
    c = two_c // 2
    ts = min(ts, s)
    vec = lambda a: a.reshape(1, c).astype(F32)
    const = lambda shape: pl.BlockSpec(shape, lambda i, t: (0, 0))
    return pl.pallas_call(
        functools.partial(_rglru_kernel, ts=ts),
        grid=(b, s // ts),
        in_specs=[pl.BlockSpec((1, ts, c), lambda i, t: (i, t, 0)),
                  pl.BlockSpec((1, ts, c), lambda i, t: (i, t, 1)),
                  const((CONV_WIDTH, c)), const((1, c)), const((c, c)), const((1, c)),
                  const((c, c)), const((1, c)), const((1, c))],
        out_specs=pl.BlockSpec((1, ts, c), lambda i, t: (i, t, 0)),
        out_shape=jax.ShapeDtypeStruct((b, s, c), BF16),
        scratch_shapes=[pltpu.VMEM((8, c), F32), pltpu.VMEM((8, c), F32)],
        compiler_params=_cparams(("arbitrary", "arbitrary")),
        name="rglru_mixer",
    )(zn, zn, conv_w.astype(F32), vec(conv_b), wr_bd, vec(br), wi_bd, vec(bi), vec(lam))


def _block_diag(w):
    n, c, d = w.shape
    eye = jnp.eye(n, dtype=w.dtype)
    return (eye[:, None, :, None] * w[:, :, None, :]).reshape(n * c, n * d)


def _swa_kernel(qg_ref, kp_ref, kc_ref, vp_ref, vc_ref, sink_ref, o_ref):
    j = pl.program_id(1)
    row = lax.broadcasted_iota(jnp.int32, (BLK, BLK), 0)
    col = lax.broadcasted_iota(jnp.int32, (BLK, BLK), 1)
    bias_c = jnp.where(row <= col, 0.0, NEG_BIG)
    bias_p = jnp.where(row > col, 0.0, NEG_BIG)
    bias_p = jnp.where(j > 0, bias_p, NEG_BIG)
    bias_c = jnp.concatenate([bias_c] * GROUP, axis=1)
    bias_p = jnp.concatenate([bias_p] * GROUP, axis=1)
    kp = kp_ref[0]
    kc = kc_ref[0]
    outs = []
    for g in range(KV_HEADS):
        q = qg_ref[0, 0, :, g * GROUP * BLK:(g + 1) * GROUP * BLK]
        lg_p =jnp.dot(kp, q, preferred_element_type=F32) + bias_p
        lg_c = jnp.dot(kc, q, preferred_element_type=F32) + bias_c
        sink = sink_ref[g] * LOG2E
        m = jnp.maximum(jnp.maximum(jnp.max(lg_p, axis=0, keepdims=True), jnp.max(lg_c, axis=0, keepdims=True)), sink)
        p_p = jnp.exp2(lg_p - m)
        p_c = jnp.exp2(lg_c - m)
        den = jnp.sum(p_p, axis=0, keepdims=True) + jnp.sum(p_c, axis=0, keepdims=True) + jnp.exp2(sink - m)
        rows = slice(g * HEAD_DIM, (g + 1) * HEAD_DIM)
        o = (jnp.dot(vp_ref[0, rows, :], p_p.astype(BF16), preferred_element_type=F32)
             + jnp.dot(vc_ref[0, rows, :], p_c.astype(BF16), preferred_element_type=F32))
        outs.append(o / den)
    for pair in range(Q_HEADS // 2):
        g, hh = divmod(2 * pair, GROUP)
        o = outs[g]
        two = jnp.concatenate([o[:, hh * BLK:(hh + 1) * BLK], o[:, (hh + 1) * BLK:(hh + 2) * BLK]], axis=0)
        o_ref[0, :, pair * BLK:(pair + 1) * BLK] = two.T.astype(o_ref.dtype)


def swa_attention(qg, k, vt, sinks):
    b, s, _ = k.shape
    nqb = s // BLK
    sink_rows = jnp.repeat(sinks.astype(F32), BLK).reshape(KV_HEADS, 1, GROUP * BLK)
    prev = lambda j: jnp.maximum(j - 1, 0)
    return pl.pallas_call(
        _swa_kernel,
        grid=(b, nqb),
        in_specs=[
            pl.BlockSpec((1, 1, BLK, Q_HEADS * BLK), lambda i, j: (i, j, 0, 0)),
            pl.BlockSpec((1, BLK, KV_HEADS * HEAD_DIM), lambda i, j: (i, prev(j), 0)),
            pl.BlockSpec((1, BLK, KV_HEADS * HEAD_DIM), lambda i, j: (i, j, 0)),
            pl.BlockSpec((1, KV_HEADS * HEAD_DIM, BLK), lambda i, j: (i, 0, prev(j))),
            pl.BlockSpec((1, KV_HEADS * HEAD_DIM, BLK), lambda i, j: (i, 0, j)),
            pl.BlockSpec((KV_HEADS, 1, GROUP * BLK), lambda i, j: (0, 0, 0)),
        ],
        out_specs=pl.BlockSpec((1, BLK, Q_HEADS * HEAD_DIM), lambda i, j: (i, j, 0)),
        out_shape=jax.ShapeDtypeStruct((b, s, Q_HEADS * HEAD_DIM), BF16),
        compiler_params=_cparams(("arbitrary", "arbitrary")),
        name="swa_attention",
    )(qg, k, k, vt, vt, sink_rows)


def _mlp_kernel(x_ref, oa_ref, ob_ref, wo_ref, g_ref, wu_ref, wd_ref, out_ref, *, hc):
    half = oa_ref.shape[1]
    x1 = (x_ref[...] + jnp.dot(oa_ref[...], wo_ref[:half, :], preferred_element_type=F32)
          + jnp.dot(ob_ref[...], wo_ref[half:, :], preferred_element_type=F32))
    ms = jnp.mean(x1 * x1, axis=-1, keepdims=True)
    hn = ((x1 * lax.rsqrt(ms + RMS_EPS)) * g_ref[...]).astype(BF16)
    out_ref[...] = x1
    hidden = wu_ref.shape[1]
    for c in range(hidden // hc):
        up = jnp.dot(hn, wu_ref[:, c * hc:(c + 1) * hc], preferred_element_type=F32)
        act = jnp.square(jnp.maximum(up, 0.0)).astype(BF16)
        out_ref[...] += jnp.dot(act, wd_ref[c * hc:(c + 1) * hc, :], preferred_element_type=F32)


def out_proj_mlp(x, oa, ob, w_out, gain, w_up, w_down, *, tm=512, hc=1024):
    b, s, d = x.shape
    t = b * s
    tm = min(tm, t)
    half = oa.shape[-1]
    hidden = w_up.shape[1]
    const = lambda shape: pl.BlockSpec(shape, lambda i: (0, 0), pipeline_mode=pl.Buffered(1))
    out = pl.pallas_call(
        functools.partial(_mlp_kernel, hc=hc),
        grid=(t // tm,),
        in_specs=[pl.BlockSpec((tm, d), lambda i: (i, 0)),
                  pl.BlockSpec((tm, half), lambda i: (i, 0)),
                  pl.BlockSpec((tm, half), lambda i: (i, 0)),
                  const((d, d)), const((1, d)), const((d, hidden)), const((hidden, d))],
        out_specs=pl.BlockSpec((tm, d), lambda i: (i, 0)),
        out_shape=jax.ShapeDtypeStruct((t, d), F32),
        compiler_params=_cparams(("arbitrary",)),
        name="out_proj_mlp",
    )(x.reshape(t, d), oa.reshape(t, half), ob.reshape(t, half), w_out.astype(BF16),
      gain.reshape(1, d).astype(F32), w_up.astype(BF16), w_down.astype(BF16))
    return out.reshape(b, s, d)


N_EVEN_T = Q_HEADS * HEAD_DIM + 2 * KV_HEADS * HEAD_DIM + IDX_HEADS * HEAD_DIM + HEAD_DIM + IDX_HEADS
N_ODD_NAT = 1024


def layer0_attention(x, pos, inp, j, rope=None):
    cos_t, sin_t = rope if rope is not None else rope_tables_t(pos)
    w_in = inp['even_w_in'][j]
    w_t = w_in[:, :N_EVEN_T].T.astype(BF16)
    w_n = w_in[:, N_EVEN_T:].astype(BF16)
    qg, k, vt, iq, ik, iw, zb = input_projection(
        x, inp['norm_mix_g'][2 * j], w_t, w_n, cos_t, sin_t, inp['a_q_norm_g'][j], inp['a_k_norm_g'][j],
        with_indexer=True)
    return dsa_attention(qg, k, vt, iq, ik, iw), zb


def layer1_mixers(x, pos, inp, j, rope=None):
    cos_t, sin_t = rope if rope is not None else rope_tables_t(pos)
    w_in = inp['odd_w_in'][j]
    w_n = w_in[:, :N_ODD_NAT].astype(BF16)
    w_t = w_in[:, N_ODD_NAT:].T.astype(BF16)
    qg, k, vt, zn = input_projection(
        x, inp['norm_mix_g'][2 * j + 1], w_t, w_n, cos_t, sin_t, inp['d_q_norm_g'][j], inp['d_k_norm_g'][j],
        with_indexer=False)
    o_c = rglru_mixer(zn, inp['c_conv_w'][j], inp['c_conv_b'][j],
                      _block_diag(inp['c_rgate_w'][j]).astype(BF16), inp['c_rgate_b'][j],
                      _block_diag(inp['c_igate_w'][j]).astype(BF16), inp['c_igate_b'][j], inp['c_lambda'][j])
    o_d = swa_attention(qg, k, vt, inp['d_sinks'][j])
    return o_c, o_d


def kernel(x, positions, norm_mix_g, norm_mlp_g, even_w_in, even_w_out, a_q_norm_g, a_k_norm_g, b_lb_logits,
           b_out_norm_g, odd_w_in, odd_w_out, c_conv_w, c_conv_b, c_rgate_w, c_rgate_b, c_igate_w, c_igate_b,
           c_lambda, d_q_norm_g, d_k_norm_g, d_sinks, mlp_w_up, mlp_w_down):
    inp = dict(norm_mix_g=norm_mix_g, even_w_in=even_w_in, a_q_norm_g=a_q_norm_g, a_k_norm_g=a_k_norm_g,
               odd_w_in=odd_w_in, c_conv_w=c_conv_w, c_conv_b=c_conv_b, c_rgate_w=c_rgate_w, c_rgate_b=c_rgate_b,
               c_igate_w=c_igate_w, c_igate_b=c_igate_b, c_lambda=c_lambda, d_q_norm_g=d_q_norm_g,
               d_k_norm_g=d_k_norm_g, d_sinks=d_sinks)
    depth = norm_mix_g.shape[0]
    rope = rope_tables_t(positions)
    for layer in range(depth):
        j = layer // 2
        if layer % 2 == 0:
            o_a, zb = layer0_attention(x, positions, inp, j, rope)
            o_b = hgrn2_mixer(zb, b_lb_logits, b_out_norm_g[j], j)
            x = out_proj_mlp(x, o_a, o_b, even_w_out[j], norm_mlp_g[layer], mlp_w_up[layer], mlp_w_down[layer])
        else:
            o_c, o_d = layer1_mixers(x, positions, inp, j, rope)
            x = out_proj_mlp(x, o_c, o_d, odd_w_out[j], norm_mlp_g[layer], mlp_w_up[layer], mlp_w_down[layer])
    return x
```

```python
import functools
import math

import numpy as np
import jax
import jax.numpy as jnp
from jax import lax
from jax.experimental import pallas as pl
from jax.experimental.pallas import tpu as pltpu

F32 = jnp.float32
BF16 = jnp.bfloat16

HEAD_DIM = 64
ROPE_THETA = 10000.0
RMS_EPS = 1e-6
KV_HEADS = 2
Q_HEADS = 8
GROUP = Q_HEADS // KV_HEADS
IDX_HEADS = 8
DSA_TOPK = 256
BLK = 128
B_HEAD_DIM = 128
B_HEADS = 4
B_CHUNK = 64
B_SUB = 16
CONV_WIDTH = 4
RG_C = 8.0
C_BLOCKS = 8
NEG_BIG = -1e30
LOG2E = math.log2(math.e)
INT_MIN = -2 ** 31

VMEM_LIMIT = 56 * 1024 * 1024


def _cparams(sem):
    return pltpu.CompilerParams(dimension_semantics=sem, vmem_limit_bytes=VMEM_LIMIT)


def _rope_kernel(pos_ref, invf_ref, cos_ref, sin_ref):
    ang = pos_ref[0].astype(F32) * invf_ref[...]
    cos_ref[0] = jnp.cos(ang)
    sin_ref[0] = jnp.sin(ang)


def rope_tables_t(positions):
    b, s = positions.shape
    half = HEAD_DIM // 2
    invf = (1.0 / (ROPE_THETA ** (jnp.arange(0, HEAD_DIM, 2, dtype=F32) / HEAD_DIM))).reshape(half, 1)
    out = jax.ShapeDtypeStruct((b, half, s), F32)
    return pl.pallas_call(
        _rope_kernel,
        grid=(b,),
        in_specs=[pl.BlockSpec((1, 1, s), lambda i: (i, 0, 0)),
                  pl.BlockSpec((half, 1), lambda i: (0, 0))],
        out_specs=[pl.BlockSpec((1, half, s), lambda i: (i, 0, 0))] * 2,
        out_shape=[out, out],
        compiler_params=_cparams(("arbitrary",)),
        name="rope_tables",
    )(positions.reshape(b, 1, s), invf)


def _head_norm_rope_t(blk, gain_col, cos, sin, scale):
    if gain_col is not None:
        ms = jnp.mean(blk * blk, axis=0, keepdims=True)
        blk = (blk * lax.rsqrt(ms + RMS_EPS)) * gain_col
    half = HEAD_DIM // 2
    x1, x2 = blk[:half], blk[half:]
    o1 = x1 * cos - x2 * sin
    o2 = x2 * cos + x1 * sin
    out = jnp.concatenate([o1, o2], axis=0)
    if scale != 1.0:
        out = out * scale
    return out


def _proj_kernel(*refs, tm, with_indexer, n_nat):
    if with_indexer:
        (x_ref, g_ref, wt_ref, wn_ref, cos_ref, sin_ref, qgain_ref, kgain_ref,
         qg_ref, k_ref, vt_ref, iq_ref, ik_ref, iw_ref, nat_ref) = refs
    else:
        (x_ref, g_ref, wt_ref, wn_ref, cos_ref, sin_ref, qgain_ref, kgain_ref,
         qg_ref, k_ref, vt_ref, nat_ref) = refs
    nq = tm // BLK
    x = x_ref[0]
    ms = jnp.mean(x * x, axis=-1, keepdims=True)
    hn = ((x * lax.rsqrt(ms + RMS_EPS)) * g_ref[...]).astype(BF16)
    zt = lax.dot_general(wt_ref[...], hn, (((1,), (1,)), ((), ())), preferred_element_type=F32)
    nat_ref[0] = jnp.dot(hn, wn_ref[...], preferred_element_type=F32)
    cos = cos_ref[0]
    sin = sin_ref[0]
    qgain = qgain_ref[...]
    kgain = kgain_ref[...]
    zeros64 = jnp.zeros((HEAD_DIM, BLK), BF16)
    r = 0
    for h in range(Q_HEADS):
        g, hh = divmod(h, GROUP)
        q = _head_norm_rope_t(zt[r:r + HEAD_DIM], qgain, cos, sin, HEAD_DIM ** -0.5 * LOG2E).astype(BF16)
        r += HEAD_DIM
        for j in range(nq):
            qg_ref[0, j, g * HEAD_DIM:(g + 1) * HEAD_DIM, h * BLK:(h + 1) * BLK] = q[:, j * BLK:(j + 1) * BLK]
            qg_ref[0, j, (1 - g) * HEAD_DIM:(2 - g) * HEAD_DIM, h * BLK:(h + 1) * BLK] = zeros64
    ks = []
    for g in range(KV_HEADS):
        ks.append(_head_norm_rope_t(zt[r:r + HEAD_DIM], kgain, cos, sin, 1.0))
        r += HEAD_DIM
    k_ref[0] = jnp.concatenate(ks, axis=0).T.astype(BF16)
    vt_ref[0] = zt[r:r + KV_HEADS * HEAD_DIM].astype(BF16)
    r += KV_HEADS * HEAD_DIM
    if with_indexer:
        for h in range(IDX_HEADS):
            iq = _head_norm_rope_t(zt[r:r + HEAD_DIM], None, cos, sin, 1.0).astype(BF16)
            r += HEAD_DIM
            for j in range(nq):
                iq_ref[0, j, 0:HEAD_DIM, h * BLK:(h + 1) * BLK] = iq[:, j * BLK:(j + 1) * BLK]
                iq_ref[0, j, HEAD_DIM:2 * HEAD_DIM, h * BLK:(h + 1) * BLK] = zeros64
        ikt = _head_norm_rope_t(zt[r:r + HEAD_DIM], None, cos, sin, 1.0)
        r += HEAD_DIM
        ik_ref[0] = jnp.concatenate([ikt, jnp.zeros_like(ikt)], axis=0).T.astype(BF16)
        iw_ref[0] = zt[r:r + IDX_HEADS] * (HEAD_DIM ** -0.5 * IDX_HEADS ** -0.5)
        r += IDX_HEADS


def input_projection(x, gain, w_t, w_n, cos_t, sin_t, q_gain, k_gain, *, with_indexer, tm=512):
    b, s, d = x.shape
    tm = min(tm, s)
    nt = w_t.shape[0]
    n_nat = w_n.shape[1]
    nqb = s // BLK
    half = HEAD_DIM // 2
    grid = (b, s // tm)
    in_specs = [
        pl.BlockSpec((1, tm, d), lambda i, j: (i, j, 0)),
        pl.BlockSpec((1, d), lambda i, j: (0, 0)),
        pl.BlockSpec((nt, d), lambda i, j: (0, 0)),
        pl.BlockSpec((d, n_nat), lambda i, j: (0, 0)),
        pl.BlockSpec((1, half, tm), lambda i, j: (i, 0, j)),
        pl.BlockSpec((1, half, tm), lambda i, j: (i, 0, j)),
        pl.BlockSpec((HEAD_DIM, 1), lambda i, j: (0, 0)),
        pl.BlockSpec((HEAD_DIM, 1), lambda i, j: (0, 0)),
    ]
    out_shape = [
        jax.ShapeDtypeStruct((b, nqb, BLK, Q_HEADS * BLK), BF16),
        jax.ShapeDtypeStruct((b, s, KV_HEADS * HEAD_DIM), BF16),
        jax.ShapeDtypeStruct((b, KV_HEADS * HEAD_DIM, s), BF16),
    ]
    out_specs = [
        pl.BlockSpec((1, tm // BLK, BLK, Q_HEADS * BLK), lambda i, j: (i, j, 0, 0)),
        pl.BlockSpec((1, tm, KV_HEADS * HEAD_DIM), lambda i, j: (i, j, 0)),
        pl.BlockSpec((1, KV_HEADS * HEAD_DIM, tm), lambda i, j: (i, 0, j)),
    ]
    if with_indexer:
        out_shape += [
            jax.ShapeDtypeStruct((b, nqb, BLK, IDX_HEADS * BLK), BF16),
            jax.ShapeDtypeStruct((b, s, BLK), BF16),
            jax.ShapeDtypeStruct((b, IDX_HEADS, s), F32),
        ]
        out_specs += [
            pl.BlockSpec((1, tm // BLK, BLK, IDX_HEADS * BLK), lambda i, j: (i, j, 0, 0)),
            pl.BlockSpec((1, tm, BLK), lambda i, j: (i, j, 0)),
            pl.BlockSpec((1, IDX_HEADS, tm), lambda i, j: (i, 0, j)),
        ]
    out_shape.append(jax.ShapeDtypeStruct((b, s, n_nat), F32))
    out_specs.append(pl.BlockSpec((1, tm, n_nat), lambda i, j: (i, j, 0)))
    return pl.pallas_call(
        functools.partial(_proj_kernel, tm=tm, with_indexer=with_indexer, n_nat=n_nat),
        grid=grid,
        in_specs=in_specs,
        out_specs=out_specs,
        out_shape=out_shape,
        compiler_params=_cparams(("arbitrary", "arbitrary")),
        name="in_proj_dsa" if with_indexer else "in_proj_swa",
    )(x, gain.reshape(1, d), w_t, w_n, cos_t, sin_t, q_gain.reshape(HEAD_DIM, 1), k_gain.reshape(HEAD_DIM, 1))


PAIR = 2 * BLK
QUAD = 4 * BLK
COUNT_ROWS = 64
FLT_LOWEST = float(np.finfo(np.float32).min)
SEARCH_FIRST_STEPS = 12
SEARCH_STEPS_PER_CHECK = 4
SEARCH_MAX_STEPS = 20


def _fold8(x):
    return x.reshape(x.shape[0] // 8, 8, x.shape[1])


def _pair_loop(npair, pair_fn, carry):
    start = 0
    for width in (4, 2, 1):
        trips = (npair - start) // width

        def body(i, c, start=start, width=width):
            for u in range(width):
                c = pair_fn(start + width * i + u, c)
            return c
        carry = lax.fori_loop(0, trips, body, carry)
        start = start + width * trips
    return carry


def _count(sc_ref, nquad, thr):
    def body(i, acc):
        off = pl.multiple_of(i * QUAD, QUAD)
        hit = jnp.where(sc_ref[pl.ds(off, QUAD), :] >= thr, 1.0, 0.0)
        return acc + jnp.sum(hit.reshape(QUAD // COUNT_ROWS, COUNT_ROWS, BLK), axis=0)
    acc = lax.fori_loop(0, nquad, body, jnp.zeros((COUNT_ROWS, BLK), F32))
    return jnp.sum(acc, axis=0, keepdims=True)


def _resolve_surplus(sc_ref, nquad, thr, excess):
    def smallest_selected():
        def min_body(i, acc):
            off = pl.multiple_of(i * QUAD, QUAD)
            blk = sc_ref[pl.ds(off, QUAD), :]
            return jnp.minimum(acc, jnp.min(_fold8(jnp.where(blk >= thr, blk, jnp.inf)), axis=0))
        low = jnp.min(lax.fori_loop(0, nquad, min_body, jnp.full((8, BLK), jnp.inf, F32)), axis=0, keepdims=True)

        def cnt_body(i, acc):
            off = pl.multiple_of(i * QUAD, QUAD)
            hit = jnp.where(sc_ref[pl.ds(off, QUAD), :] == low, 1.0, 0.0)
            return acc + jnp.sum(hit.reshape(QUAD // COUNT_ROWS, COUNT_ROWS, BLK), axis=0)
        copies = jnp.sum(lax.fori_loop(0, nquad, cnt_body, jnp.zeros((COUNT_ROWS, BLK), F32)), axis=0, keepdims=True)
        return low, copies

    def cond(st):
        ex, _, copies = st
        return jnp.max(jnp.where(ex > copies, 1.0, 0.0)) > 0.0

    def body(st):
        ex, low, copies = st
        whole = ex > copies
        gone = jnp.where(whole, low, jnp.nan)

        def drop_body(i, carry):
            off = pl.multiple_of(i * QUAD, QUAD)
            blk = sc_ref[pl.ds(off, QUAD), :]
            sc_ref[pl.ds(off, QUAD), :] = jnp.where(blk == gone, -jnp.inf, blk)
            return carry
        lax.fori_loop(0, nquad, drop_body, 0)
        ex = jnp.where(whole, ex - copies, ex)
        low, copies = smallest_selected()
        return ex, low, copies

    low, copies = smallest_selected()
    ex, low, copies = lax.while_loop(cond, body, (excess, low, copies))
    surplus = ex > 0.0
    return jnp.where(surplus, low, jnp.nan), jnp.where(surplus, copies - ex, 0.0)


def _search_step(sc_ref, nquad, topk, st):
    lo, hi, c_lo, c_hi, done = st
    guess = lo + (hi - lo) * 0.5
    inside = (guess > lo) & (guess < hi)
    c = _count(sc_ref, nquad, guess)
    move = inside & (done == 0.0)
    up = move & (c >= topk)
    down = move & (c < topk)
    lo = jnp.where(up, guess, lo)
    c_lo = jnp.where(up, c, c_lo)
    hi = jnp.where(down, guess, hi)
    c_hi = jnp.where(down, c, c_hi)
    done = jnp.where(inside & (c_lo != topk), done, 1.0)
    return lo, hi, c_lo, c_hi, done


def _dsa_kernel(q_ref, k_ref, vt_ref, iq_ref, ik_ref, iw_ref, o_ref, sc_ref, lg_ref, acc_ref, mx_ref, tri_ref,
                *, topk):
    qi = pl.program_id(1)
    npair = (qi + 2) // 2
    iq = iq_ref[0, 0]
    w = iw_ref[0]
    nquad = (qi + 4) // 4
    row2 = lax.broadcasted_iota(jnp.int32, (PAIR, BLK), 0)
    col2 = lax.broadcasted_iota(jnp.int32, (PAIR, BLK), 1)

    def score_body(pi, carry):
        lo8, hi8 = carry
        off = pl.multiple_of(pi * PAIR, PAIR)
        ikb = ik_ref[0, pl.ds(off, PAIR), :]
        tot = jnp.zeros((PAIR, BLK), F32)
        for h in range(IDX_HEADS):
            sc = jnp.dot(ikb, iq_ref[0, 0, :, h * BLK:(h + 1) * BLK], preferred_element_type=F32)
            tot = tot + jnp.maximum(sc, 0.0) * w[h:h + 1, :]
        causal = (pi * PAIR + row2) <= (qi * BLK + col2)
        sc_ref[pl.ds(off, PAIR), :] = jnp.where(causal, tot, -jnp.inf)
        lo8 = jnp.minimum(lo8, jnp.min(_fold8(jnp.where(causal, tot, jnp.inf)), axis=0))
        hi8 = jnp.maximum(hi8, jnp.max(_fold8(jnp.where(causal, tot, -jnp.inf)), axis=0))
        return lo8, hi8

    lo8, hi8 = _pair_loop(npair, score_body,
                          (jnp.full((8, BLK), jnp.inf, F32), jnp.full((8, BLK), -jnp.inf, F32)))

    @pl.when(2 * npair < 4 * nquad)
    def _():
        sc_ref[pl.ds(pl.multiple_of(npair * PAIR, PAIR), PAIR), :] = jnp.full((PAIR, BLK), -jnp.inf, F32)

    smin = jnp.min(lo8, axis=0, keepdims=True)
    smax = jnp.max(hi8, axis=0, keepdims=True)
    n_valid = (qi * BLK + 1 + lax.broadcasted_iota(jnp.int32, (1, BLK), 1)).astype(F32)
    few = n_valid <= topk
    above = smax + (jnp.abs(smax) * 1e-6 + 1e-30)
    state = (smin, above, n_valid, jnp.zeros((1, BLK), F32), jnp.where(few, 1.0, 0.0))

    def search_cond(st):
        i, inner = st
        return (i < SEARCH_MAX_STEPS) & (jnp.min(inner[4]) == 0.0)

    def search_body(st):
        i, inner = st
        for _ in range(SEARCH_STEPS_PER_CHECK):
            inner = _search_step(sc_ref, nquad, topk, inner)
        return i + SEARCH_STEPS_PER_CHECK, inner

    _, state = lax.fori_loop(0, SEARCH_FIRST_STEPS // SEARCH_STEPS_PER_CHECK, lambda i, st: search_body(st),
                             (jnp.int32(0), state))
    _, (lo, _, c_lo, _, _) = lax.while_loop(search_cond, search_body, (jnp.int32(SEARCH_FIRST_STEPS), state))
    thr = jnp.where(few, FLT_LOWEST, lo)
    excess = jnp.where(few, 0.0, c_lo - topk)

    def logits_of_pair(off, bias, mx):
        kb = k_ref[0, pl.ds(off, PAIR), :]
        parts = []
        for h in range(Q_HEADS):
            lgh = jnp.dot(kb, q_ref[0, 0, :, h * BLK:(h + 1) * BLK], preferred_element_type=F32) + bias
            lg_ref[pl.ds(off, PAIR), h * BLK:(h + 1) * BLK] = lgh
            parts.append(jnp.max(_fold8(lgh), axis=0))
        return jnp.maximum(mx, jnp.concatenate(parts, axis=1))

    mx0 = jnp.full((8, Q_HEADS * BLK), NEG_BIG, F32)
    has_surplus = jnp.max(excess) > 0.0

    @pl.when(jnp.logical_not(has_surplus))
    def _():
        def logit_body(pi, mx):
            off = pl.multiple_of(pi * PAIR, PAIR)
            return logits_of_pair(off, jnp.where(sc_ref[pl.ds(off, PAIR), :] >= thr, 0.0, NEG_BIG), mx)
        mx_ref[...] = _pair_loop(npair, logit_body, mx0)

    @pl.when(has_surplus)
    def _():
        low, keep = _resolve_surplus(sc_ref, nquad, thr, excess)
        row = lax.broadcasted_iota(jnp.int32, (PAIR, PAIR), 0)
        col = lax.broadcasted_iota(jnp.int32, (PAIR, PAIR), 1)
        tri_ref[...] = (col < row).astype(BF16)

        def logit_body(pi, carry):
            mx, seen = carry
            off = pl.multiple_of(pi * PAIR, PAIR)
            sc = sc_ref[pl.ds(off, PAIR), :]
            eq = sc == low
            eqf = jnp.where(eq, 1.0, 0.0)
            rank = jnp.dot(tri_ref[...], eqf.astype(BF16), preferred_element_type=F32) + seen
            bias = jnp.where(sc >= thr, 0.0, NEG_BIG)
            bias = jnp.where(eq & (rank >= keep), NEG_BIG, bias)
            return logits_of_pair(off, bias, mx), seen + jnp.sum(eqf, axis=0, keepdims=True)

        mx, _ = lax.fori_loop(0, npair, logit_body, (mx0, jnp.zeros((1, BLK), F32)))
        mx_ref[...] = mx

    m = jnp.max(mx_ref[...], axis=0, keepdims=True)

    acc_ref[...] = jnp.zeros(acc_ref.shape, F32)

    def pv_body(pi, ls):
        off = pl.multiple_of(pi * PAIR, PAIR)
        sums, pbs = [], []
        for h in range(Q_HEADS):
            cs = slice(h * BLK, (h + 1) * BLK)
            p = jnp.exp2(lg_ref[pl.ds(off, PAIR), cs] - m[:, cs])
            sums.append(jnp.sum(_fold8(p), axis=0))
            pbs.append(p.astype(BF16))
        for g in range(KV_HEADS):
            pb = jnp.concatenate(pbs[g * GROUP:(g + 1) * GROUP], axis=1)
            vt = vt_ref[0, g * HEAD_DIM:(g + 1) * HEAD_DIM, pl.ds(off, PAIR)]
            acc_ref[g] += jnp.dot(vt, pb, preferred_element_type=F32)
        return ls + jnp.concatenate(sums, axis=1)

    ls = _pair_loop(npair, pv_body, jnp.zeros((8, Q_HEADS * BLK), F32))
    l = jnp.sum(ls, axis=0, keepdims=True)

    for pair in range(Q_HEADS // 2):
        g, hh = divmod(2 * pair, GROUP)
        o = acc_ref[g] / l[:, g * GROUP * BLK:(g + 1) * GROUP * BLK]
        two = jnp.concatenate([o[:, hh * BLK:(hh + 1) * BLK], o[:, (hh + 1) * BLK:(hh + 2) * BLK]], axis=0)
        o_ref[0, :, pair * BLK:(pair + 1) * BLK] = two.T.astype(o_ref.dtype)


def dsa_attention(qg, k, vt, iq, ik, iw):
    b, s, _ = k.shape
    assert s % QUAD == 0, "the counting loops pad the score tile to whole groups of four key blocks"
    nqb = s // BLK
    topk = min(DSA_TOPK, s // 4)
    return pl.pallas_call(
        functools.partial(_dsa_kernel, topk=topk),
        grid=(b, nqb),
        in_specs=[
            pl.BlockSpec((1, 1, BLK, Q_HEADS * BLK), lambda i, j: (i, j, 0, 0)),
            pl.BlockSpec((1, s, KV_HEADS * HEAD_DIM), lambda i, j: (i, 0, 0)),
            pl.BlockSpec((1, KV_HEADS * HEAD_DIM, s), lambda i, j: (i, 0, 0)),
            pl.BlockSpec((1, 1, BLK, IDX_HEADS * BLK), lambda i, j: (i, j, 0, 0)),
            pl.BlockSpec((1, s, BLK), lambda i, j: (i, 0, 0)),
            pl.BlockSpec((1, IDX_HEADS, BLK), lambda i, j: (i, 0, j)),
        ],
        out_specs=pl.BlockSpec((1, BLK, Q_HEADS * HEAD_DIM), lambda i, j: (i, j, 0)),
        out_shape=jax.ShapeDtypeStruct((b, s, Q_HEADS * HEAD_DIM), BF16),
        scratch_shapes=[
            pltpu.VMEM((s, BLK), F32),
            pltpu.VMEM((s, Q_HEADS * BLK), F32),
            pltpu.VMEM((KV_HEADS, HEAD_DIM, GROUP * BLK), F32),
            pltpu.VMEM((8, Q_HEADS * BLK), F32),
            pltpu.VMEM((PAIR, PAIR), BF16),
        ],
        compiler_params=_cparams(("arbitrary", "arbitrary")),
        name="dsa_attention",
    )(qg, k, vt, iq, ik, iw)


def _split3(x):
    hi = x.astype(BF16)
    r1 = x - hi.astype(F32)
    mid = r1.astype(BF16)
    lo = (r1 - mid.astype(F32)).astype(BF16)
    return hi, mid, lo


def _hgrn_chunk(qraw, flog, v, lb, state_t):
    c = B_CHUNK
    q = qraw * jax.nn.sigmoid(qraw)
    f = lb + (1.0 - lb) * jax.nn.sigmoid(flog)
    k = 1.0 - f
    logf = jnp.log(f)
    row = lax.broadcasted_iota(jnp.int32, (c, c), 0)
    lane = lax.broadcasted_iota(jnp.int32, (c, c), 1)
    incl = (lane <= row).astype(BF16)
    b = sum(jnp.dot(incl, part, preferred_element_type=F32) for part in _split3(logf))

    att = jnp.zeros((c, c), F32)
    sub_pos = row & (B_SUB - 1)
    kf = k
    for delta in range(B_SUB):
        if delta > 0:
            kf = pltpu.roll(kf, 1, axis=0) * f
        diag = jnp.sum(q * kf, axis=-1, keepdims=True)
        att = jnp.where((lane == row - delta) & (sub_pos >= delta), diag, att)

    rows = [jnp.zeros((B_SUB, c), F32)]
    for i in range(1, c // B_SUB):
        b_i = b[i * B_SUB - 1:i * B_SUB, :]
        kt = (k * jnp.exp(jnp.minimum(b_i - b, 0.0))).astype(BF16)
        qt = (q[i * B_SUB:(i + 1) * B_SUB] * jnp.exp(b[i * B_SUB:(i + 1) * B_SUB] - b_i)).astype(BF16)
        rows.append(lax.dot_general(qt, kt, (((1,), (1,)), ((), ())), preferred_element_type=F32))
    cross = jnp.concatenate(rows, axis=0)
    att = jnp.where((lane // B_SUB) < (row // B_SUB), cross, att)

    vb = v.astype(BF16)
    o = jnp.dot(att.astype(BF16), vb, preferred_element_type=F32)
    o = o + lax.dot_general((q * jnp.exp(b)).astype(BF16), state_t.astype(BF16), (((1,), (1,)), ((), ())),
                            preferred_element_type=F32)
    b_last = b[c - 1:c, :]
    kd = (k * jnp.exp(b_last - b)).astype(BF16)
    new_state_t = state_t * jnp.exp(b_last) + lax.dot_general(vb, kd, (((0,), (0,)), ((), ())),
                                                              preferred_element_type=F32)
    return o, new_state_t


def _hgrn_kernel(q_ref, f_ref, v_ref, g_ref, lbl_ref, gain_ref, o_ref, state_ref, *, layer, ts):
    @pl.when(pl.program_id(1) == 0)
    def _():
        state_ref[...] = jnp.zeros(state_ref.shape, F32)

    lbl = lbl_ref[...]
    e = jnp.exp(lbl - jnp.max(lbl, axis=0, keepdims=True))
    lb_all = jnp.sum(e[:layer + 1], axis=0, keepdims=True) / jnp.sum(e, axis=0, keepdims=True)
    gain = gain_ref[...]

    def chunk_body(ci, carry):
        r0 = pl.multiple_of(ci * B_CHUNK, B_CHUNK)
        for h in range(B_HEADS):
            cs = slice(h * B_HEAD_DIM, (h + 1) * B_HEAD_DIM)
            o, st = _hgrn_chunk(q_ref[0, pl.ds(r0, B_CHUNK), cs], f_ref[0, pl.ds(r0, B_CHUNK), cs],
                                v_ref[0, pl.ds(r0, B_CHUNK), cs], lb_all[:, cs], state_ref[h])
            state_ref[h] = st
            ms = jnp.mean(o * o, axis=-1, keepdims=True)
            o = (o * lax.rsqrt(ms + RMS_EPS)) * gain
            gate = g_ref[0, pl.ds(r0, B_CHUNK), cs]
            o_ref[0, pl.ds(r0, B_CHUNK), cs] = (o * (gate * jax.nn.sigmoid(gate))).astype(o_ref.dtype)
        return carry

    lax.fori_loop(0, ts // B_CHUNK, chunk_body, 0, unroll=2)


def hgrn2_mixer(zb, lb_logits, out_gain, layer, *, ts=256):
    b, s, _ = zb.shape
    width = B_HEADS * B_HEAD_DIM
    ts = min(ts, s)
    spec = lambda c: pl.BlockSpec((1, ts, width), lambda i, t, c=c: (i, t, c))
    return pl.pallas_call(
        functools.partial(_hgrn_kernel, layer=layer, ts=ts),
        grid=(b, s // ts),
        in_specs=[spec(0), spec(1), spec(2), spec(3),
                  pl.BlockSpec(lb_logits.shape, lambda i, t: (0, 0)),
                  pl.BlockSpec((1, B_HEAD_DIM), lambda i, t: (0, 0))],
        out_specs=pl.BlockSpec((1, ts, width), lambda i, t: (i, t, 0)),
        out_shape=jax.ShapeDtypeStruct((b, s, width), BF16),
        scratch_shapes=[pltpu.VMEM((B_HEADS, B_HEAD_DIM, B_HEAD_DIM), F32)],
        compiler_params=_cparams(("arbitrary", "arbitrary")),
        name="hgrn2_mixer",
    )(zb, zb, zb, zb, lb_logits.astype(F32), out_gain.reshape(1, B_HEAD_DIM).astype(F32))


def _shift_rows(x, prev8, k, row8):
    if k == 0:
        return x
    rolled = pltpu.roll(x, k, axis=0)
    head = jnp.where(row8 < k, pltpu.roll(prev8, k, axis=0), rolled[:8])
    return jnp.concatenate([head, rolled[8:]], axis=0)


def _rglru_kernel(g_ref, x_ref, cw_ref, cb_ref, wr_ref, br_ref, wi_ref, bi_ref, lam_ref, o_ref,
                  xprev_ref, hprev_ref, *, ts):
    @pl.when(pl.program_id(1) == 0)
    def _():
        xprev_ref[...] = jnp.zeros(xprev_ref.shape, F32)
        hprev_ref[...] = jnp.zeros(hprev_ref.shape, F32)

    x = x_ref[0]
    width = x.shape[1]
    prev8 = xprev_ref[...]
    row8 = lax.broadcasted_iota(jnp.int32, (8, width), 0)
    cw = cw_ref[...]
    xc = cb_ref[...] + sum(_shift_rows(x, prev8, k, row8) * cw[CONV_WIDTH - 1 - k:CONV_WIDTH - k, :]
                           for k in range(CONV_WIDTH))
    xprev_ref[...] = x[ts - 8:]

    xb = xc.astype(BF16)
    r = jax.nn.sigmoid(jnp.dot(xb, wr_ref[...], preferred_element_type=F32) + br_ref[...])
    ig = jax.nn.sigmoid(jnp.dot(xb, wi_ref[...], preferred_element_type=F32) + bi_ref[...])
    lam = lam_ref[...]
    softplus_neg = jnp.maximum(-lam, 0.0) + jnp.log(1.0 + jnp.exp(-jnp.abs(lam)))
    a = jnp.exp((-RG_C * r) * softplus_neg)
    u = jnp.sqrt(1.0 - a * a) * (ig * xc)

    pos = lax.broadcasted_iota(jnp.int32, (ts, width), 0) & 7
    d = 1
    while d < 8:
        keep = pos >= d
        u = jnp.where(keep, a * pltpu.roll(u, d, axis=0) + u, u)
        a = jnp.where(keep, a * pltpu.roll(a, d, axis=0), a)
        d *= 2
    carry = hprev_ref[0:1, :]
    hs = []
    for g in range(ts // 8):
        hg = u[8 * g:8 * g + 8] + a[8 * g:8 * g + 8] * carry
        carry = hg[7:8]
        hs.append(hg)
    h = jnp.concatenate(hs, axis=0)
    hprev_ref[...] = jnp.broadcast_to(carry, hprev_ref.shape)

    gt = g_ref[0]
    gelu = 0.5 * gt * (1.0 + jnp.tanh(np.float32(math.sqrt(2.0 / math.pi)) * (gt + 0.044715 * (gt * gt * gt))))
    o_ref[0] = (gelu * h).astype(o_ref.dtype)


def rglru_mixer(zn, conv_w, conv_b, wr_bd, br, wi_bd, bi, lam, *, ts=256):
    b, s, two_c = zn.shape
    c = two_c // 2
    ts = min(ts, s)
    vec = lambda a: a.reshape(1, c).astype(F32)
    const = lambda shape: pl.BlockSpec(shape, lambda i, t: (0, 0))
    return pl.pallas_call(
        functools.partial(_rglru_kernel, ts=ts),
        grid=(b, s // ts),
        in_specs=[pl.BlockSpec((1, ts, c), lambda i, t: (i, t, 0)),
                  pl.BlockSpec((1, ts, c), lambda i, t: (i, t, 1)),
                  const((CONV_WIDTH, c)), const((1, c)), const((c, c)), const((1, c)),
                  const((c, c)), const((1, c)), const((1, c))],
        out_specs=pl.BlockSpec((1, ts, c), lambda i, t: (i, t, 0)),
        out_shape=jax.ShapeDtypeStruct((b, s, c), BF16),
        scratch_shapes=[pltpu.VMEM((8, c), F32), pltpu.VMEM((8, c), F32)],
        compiler_params=_cparams(("arbitrary", "arbitrary")),
        name="rglru_mixer",
    )(zn, zn, conv_w.astype(F32), vec(conv_b), wr_bd, vec(br), wi_bd, vec(bi), vec(lam))


def _block_diag(w):
    n, c, d = w.shape
    eye = jnp.eye(n, dtype=w.dtype)
    return (eye[:, None, :, None] * w[:, :, None, :]).reshape(n * c, n * d)


def _swa_kernel(qg_ref, kp_ref, kc_ref, vp_ref, vc_ref, sink_ref, o_ref):
    j = pl.program_id(1)
    row = lax.broadcasted_iota(jnp.int32, (BLK, BLK), 0)
    col = lax.broadcasted_iota(jnp.int32, (BLK, BLK), 1)
    bias_c = jnp.where(row <= col, 0.0, NEG_BIG)
    bias_p = jnp.where(row > col, 0.0, NEG_BIG)
    bias_p = jnp.where(j > 0, bias_p, NEG_BIG)
    bias_c = jnp.concatenate([bias_c] * GROUP, axis=1)
    bias_p = jnp.concatenate([bias_p] * GROUP, axis=1)
    kp = kp_ref[0]
    kc = kc_ref[0]
    outs = []
    for g in range(KV_HEADS):
        q = qg_ref[0, 0, :, g * GROUP * BLK:(g + 1) * GROUP * BLK]
        lg_p =jnp.dot(kp, q, preferred_element_type=F32) + bias_p
        lg_c = jnp.dot(kc, q, preferred_element_type=F32) + bias_c
        sink = sink_ref[g] * LOG2E
        m = jnp.maximum(jnp.maximum(jnp.max(lg_p, axis=0, keepdims=True), jnp.max(lg_c, axis=0, keepdims=True)), sink)
        p_p = jnp.exp2(lg_p - m)
        p_c = jnp.exp2(lg_c - m)
        den = jnp.sum(p_p, axis=0, keepdims=True) + jnp.sum(p_c, axis=0, keepdims=True) + jnp.exp2(sink - m)
        rows = slice(g * HEAD_DIM, (g + 1) * HEAD_DIM)
        o = (jnp.dot(vp_ref[0, rows, :], p_p.astype(BF16), preferred_element_type=F32)
             + jnp.dot(vc_ref[0, rows, :], p_c.astype(BF16), preferred_element_type=F32))
        outs.append(o / den)
    for pair in range(Q_HEADS // 2):
        g, hh = divmod(2 * pair, GROUP)
        o = outs[g]
        two = jnp.concatenate([o[:, hh * BLK:(hh + 1) * BLK], o[:, (hh + 1) * BLK:(hh + 2) * BLK]], axis=0)
        o_ref[0, :, pair * BLK:(pair + 1) * BLK] = two.T.astype(o_ref.dtype)


def swa_attention(qg, k, vt, sinks):
    b, s, _ = k.shape
    nqb = s // BLK
    sink_rows = jnp.repeat(sinks.astype(F32), BLK).reshape(KV_HEADS, 1, GROUP * BLK)
    prev = lambda j: jnp.maximum(j - 1, 0)
    return pl.pallas_call(
        _swa_kernel,
        grid=(b, nqb),
        in_specs=[
            pl.BlockSpec((1, 1, BLK, Q_HEADS * BLK), lambda i, j: (i, j, 0, 0)),
            pl.BlockSpec((1, BLK, KV_HEADS * HEAD_DIM), lambda i, j: (i, prev(j), 0)),
            pl.BlockSpec((1, BLK, KV_HEADS * HEAD_DIM), lambda i, j: (i, j, 0)),
            pl.BlockSpec((1, KV_HEADS * HEAD_DIM, BLK), lambda i, j: (i, 0, prev(j))),
            pl.BlockSpec((1, KV_HEADS * HEAD_DIM, BLK), lambda i, j: (i, 0, j)),
            pl.BlockSpec((KV_HEADS, 1, GROUP * BLK), lambda i, j: (0, 0, 0)),
        ],
        out_specs=pl.BlockSpec((1, BLK, Q_HEADS * HEAD_DIM), lambda i, j: (i, j, 0)),
        out_shape=jax.ShapeDtypeStruct((b, s, Q_HEADS * HEAD_DIM), BF16),
        compiler_params=_cparams(("arbitrary", "arbitrary")),
        name="swa_attention",
    )(qg, k, k, vt, vt, sink_rows)


def _mlp_kernel(x_ref, oa_ref, ob_ref, wo_ref, g_ref, wu_ref, wd_ref, out_ref, *, hc):
    half = oa_ref.shape[1]
    x1 = (x_ref[...] + jnp.dot(oa_ref[...], wo_ref[:half, :], preferred_element_type=F32)
          + jnp.dot(ob_ref[...], wo_ref[half:, :], preferred_element_type=F32))
    ms = jnp.mean(x1 * x1, axis=-1, keepdims=True)
    hn = ((x1 * lax.rsqrt(ms + RMS_EPS)) * g_ref[...]).astype(BF16)
    out_ref[...] = x1
    hidden = wu_ref.shape[1]
    for c in range(hidden // hc):
        up = jnp.dot(hn, wu_ref[:, c * hc:(c + 1) * hc], preferred_element_type=F32)
        act = jnp.square(jnp.maximum(up, 0.0)).astype(BF16)
        out_ref[...] += jnp.dot(act, wd_ref[c * hc:(c + 1) * hc, :], preferred_element_type=F32)


def out_proj_mlp(x, oa, ob, w_out, gain, w_up, w_down, *, tm=512, hc=1024):
    b, s, d = x.shape
    t = b * s
    tm = min(tm, t)
    half = oa.shape[-1]
    hidden = w_up.shape[1]
    const = lambda shape: pl.BlockSpec(shape, lambda i: (0, 0), pipeline_mode=pl.Buffered(1))
    out = pl.pallas_call(
        functools.partial(_mlp_kernel, hc=hc),
        grid=(t // tm,),
        in_specs=[pl.BlockSpec((tm, d), lambda i: (i, 0)),
                  pl.BlockSpec((tm, half), lambda i: (i, 0)),
                  pl.BlockSpec((tm, half), lambda i: (i, 0)),
                  const((d, d)), const((1, d)), const((d, hidden)), const((hidden, d))],
        out_specs=pl.BlockSpec((tm, d), lambda i: (i, 0)),
        out_shape=jax.ShapeDtypeStruct((t, d), F32),
        compiler_params=_cparams(("arbitrary",)),
        name="out_proj_mlp",
    )(x.reshape(t, d), oa.reshape(t, half), ob.reshape(t, half), w_out.astype(BF16),
      gain.reshape(1, d).astype(F32), w_up.astype(BF16), w_down.astype(BF16))
    return out.reshape(b, s, d)


N_EVEN_T = Q_HEADS * HEAD_DIM + 2 * KV_HEADS * HEAD_DIM + IDX_HEADS * HEAD_DIM + HEAD_DIM + IDX_HEADS
N_ODD_NAT = 1024


def layer0_attention(x, pos, inp, j, rope=None):
    cos_t, sin_t = rope if rope is not None else rope_tables_t(pos)
    w_in = inp['even_w_in'][j]
    w_t = w_in[:, :N_EVEN_T].T.astype(BF16)
    w_n = w_in[:, N_EVEN_T:].astype(BF16)
    qg, k, vt, iq, ik, iw, zb = input_projection(
        x, inp['norm_mix_g'][2 * j], w_t, w_n, cos_t, sin_t, inp['a_q_norm_g'][j], inp['a_k_norm_g'][j],
        with_indexer=True)
    return dsa_attention(qg, k, vt, iq, ik, iw), zb


def layer1_mixers(x, pos, inp, j, rope=None):
    cos_t, sin_t = rope if rope is not None else rope_tables_t(pos)
    w_in = inp['odd_w_in'][j]
    w_n = w_in[:, :N_ODD_NAT].astype(BF16)
    w_t = w_in[:, N_ODD_NAT:].T.astype(BF16)
    qg, k, vt, zn = input_projection(
        x, inp['norm_mix_g'][2 * j + 1], w_t, w_n, cos_t, sin_t, inp['d_q_norm_g'][j], inp['d_k_norm_g'][j],
        with_indexer=False)
    o_c = rglru_mixer(zn, inp['c_conv_w'][j], inp['c_conv_b'][j],
                      _block_diag(inp['c_rgate_w'][j]).astype(BF16), inp['c_rgate_b'][j],
                      _block_diag(inp['c_igate_w'][j]).astype(BF16), inp['c_igate_b'][j], inp['c_lambda'][j])
    o_d = swa_attention(qg, k, vt, inp['d_sinks'][j])
    return o_c, o_d


def kernel(x, positions, norm_mix_g, norm_mlp_g, even_w_in, even_w_out, a_q_norm_g, a_k_norm_g, b_lb_logits,
           b_out_norm_g, odd_w_in, odd_w_out, c_conv_w, c_conv_b, c_rgate_w, c_rgate_b, c_igate_w, c_igate_b,
           c_lambda, d_q_norm_g, d_k_norm_g, d_sinks, mlp_w_up, mlp_w_down):
    inp = dict(norm_mix_g=norm_mix_g, even_w_in=even_w_in, a_q_norm_g=a_q_norm_g, a_k_norm_g=a_k_norm_g,
               odd_w_in=odd_w_in, c_conv_w=c_conv_w, c_conv_b=c_conv_b, c_rgate_w=c_rgate_w, c_rgate_b=c_rgate_b,
               c_igate_w=c_igate_w, c_igate_b=c_igate_b, c_lambda=c_lambda, d_q_norm_g=d_q_norm_g,
               d_k_norm_g=d_k_norm_g, d_sinks=d_sinks)
    depth = norm_mix_g.shape[0]
    rope = rope_tables_t(positions)
    for layer in range(depth):
        j = layer // 2
        if layer % 2 == 0:
            o_a, zb = layer0_attention(x, positions, inp, j, rope)
            o_b = hgrn2_mixer(zb, b_lb_logits, b_out_norm_g[j], j)
            x = out_proj_mlp(x, o_a, o_b, even_w_out[j], norm_mlp_g[layer], mlp_w_up[layer], mlp_w_down[layer])
        else:
            o_c, o_d = layer1_mixers(x, positions, inp, j, rope)
            x = out_proj_mlp(x, o_c, o_d, odd_w_out[j], norm_mlp_g[layer], mlp_w_up[layer], mlp_w_down[layer])
    return x
```

```python
import functools
import math

import numpy as np
import jax
import jax.numpy as jnp
from jax import lax
from jax.experimental import pallas as pl
from jax.experimental.pallas import tpu as pltpu

F32 = jnp.float32
BF16 = jnp.bfloat16

HEAD_DIM = 64
ROPE_THETA = 10000.0
RMS_EPS = 1e-6
KV_HEADS = 2
Q_HEADS = 8
GROUP = Q_HEADS // KV_HEADS
IDX_HEADS = 8
DSA_TOPK = 256
BLK = 128
B_HEAD_DIM = 128
B_HEADS = 4
B_CHUNK = 64
B_SUB = 16
CONV_WIDTH = 4
RG_C = 8.0
C_BLOCKS = 8
NEG_BIG = -1e30
LOG2E = math.log2(math.e)
INT_MIN = -2 ** 31

VMEM_LIMIT = 56 * 1024 * 1024


def _cparams(sem):
    return pltpu.CompilerParams(dimension_semantics=sem, vmem_limit_bytes=VMEM_LIMIT)


def _rope_kernel(pos_ref, invf_ref, cos_ref, sin_ref):
    ang = pos_ref[0].astype(F32) * invf_ref[...]
    cos_ref[0] = jnp.cos(ang)
    sin_ref[0] = jnp.sin(ang)


def rope_tables_t(positions):
    b, s = positions.shape
    half = HEAD_DIM // 2
    invf = (1.0 / (ROPE_THETA ** (jnp.arange(0, HEAD_DIM, 2, dtype=F32) / HEAD_DIM))).reshape(half, 1)
    out = jax.ShapeDtypeStruct((b, half, s), F32)
    return pl.pallas_call(
        _rope_kernel,
        grid=(b,),
        in_specs=[pl.BlockSpec((1, 1, s), lambda i: (i, 0, 0)),
                  pl.BlockSpec((half, 1), lambda i: (0, 0))],
        out_specs=[pl.BlockSpec((1, half, s), lambda i: (i, 0, 0))] * 2,
        out_shape=[out, out],
        compiler_params=_cparams(("arbitrary",)),
        name="rope_tables",
    )(positions.reshape(b, 1, s), invf)


def _head_norm_rope_t(blk, gain_col, cos, sin, scale):
    if gain_col is not None:
        ms = jnp.mean(blk * blk, axis=0, keepdims=True)
        blk = (blk * lax.rsqrt(ms + RMS_EPS)) * gain_col
    half = HEAD_DIM // 2
    x1, x2 = blk[:half], blk[half:]
    o1 = x1 * cos - x2 * sin
    o2 = x2 * cos + x1 * sin
    out = jnp.concatenate([o1, o2], axis=0)
    if scale != 1.0:
        out = out * scale
    return out


def _proj_kernel(*refs, tm, with_indexer, n_nat):
    if with_indexer:
        (x_ref, g_ref, wt_ref, wn_ref, cos_ref, sin_ref, qgain_ref, kgain_ref,
         qg_ref, k_ref, vt_ref, iq_ref, ik_ref, iw_ref, nat_ref) = refs
    else:
        (x_ref, g_ref, wt_ref, wn_ref, cos_ref, sin_ref, qgain_ref, kgain_ref,
         qg_ref, k_ref, vt_ref, nat_ref) = refs
    nq = tm // BLK
    x = x_ref[0]
    ms = jnp.mean(x * x, axis=-1, keepdims=True)
    hn = ((x * lax.rsqrt(ms + RMS_EPS)) * g_ref[...]).astype(BF16)
    zt = lax.dot_general(wt_ref[...], hn, (((1,), (1,)), ((), ())), preferred_element_type=F32)
    nat_ref[0] = jnp.dot(hn, wn_ref[...], preferred_element_type=F32)
    cos = cos_ref[0]
    sin = sin_ref[0]
    qgain = qgain_ref[...]
    kgain = kgain_ref[...]
    zeros64 = jnp.zeros((HEAD_DIM, BLK), BF16)
    r = 0
    for h in range(Q_HEADS):
        g, hh = divmod(h, GROUP)
        q = _head_norm_rope_t(zt[r:r + HEAD_DIM], qgain, cos, sin, HEAD_DIM ** -0.5 * LOG2E).astype(BF16)
        r += HEAD_DIM
        for j in range(nq):
            qg_ref[0, j, g * HEAD_DIM:(g + 1) * HEAD_DIM, h * BLK:(h + 1) * BLK] = q[:, j * BLK:(j + 1) * BLK]
            qg_ref[0, j, (1 - g) * HEAD_DIM:(2 - g) * HEAD_DIM, h * BLK:(h + 1) * BLK] = zeros64
    ks = []
    for g in range(KV_HEADS):
        ks.append(_head_norm_rope_t(zt[r:r + HEAD_DIM], kgain, cos, sin, 1.0))
        r += HEAD_DIM
    k_ref[0] = jnp.concatenate(ks, axis=0).T.astype(BF16)
    vt_ref[0] = zt[r:r + KV_HEADS * HEAD_DIM].astype(BF16)
    r += KV_HEADS * HEAD_DIM
    if with_indexer:
        for h in range(IDX_HEADS):
            iq = _head_norm_rope_t(zt[r:r + HEAD_DIM], None, cos, sin, 1.0).astype(BF16)
            r += HEAD_DIM
            for j in range(nq):
                iq_ref[0, j, 0:HEAD_DIM, h * BLK:(h + 1) * BLK] = iq[:, j * BLK:(j + 1) * BLK]
                iq_ref[0, j, HEAD_DIM:2 * HEAD_DIM, h * BLK:(h + 1) * BLK] = zeros64
        ikt = _head_norm_rope_t(zt[r:r + HEAD_DIM], None, cos, sin, 1.0)
        r += HEAD_DIM
        ik_ref[0] = jnp.concatenate([ikt, jnp.zeros_like(ikt)], axis=0).T.astype(BF16)
        iw_ref[0] = zt[r:r + IDX_HEADS] * (HEAD_DIM ** -0.5 * IDX_HEADS ** -0.5)
        r += IDX_HEADS


def input_projection(x, gain, w_t, w_n, cos_t, sin_t, q_gain, k_gain, *, with_indexer, tm=512):
    b, s, d = x.shape
    tm = min(tm, s)
    nt = w_t.shape[0]
    n_nat = w_n.shape[1]
    nqb = s // BLK
    half = HEAD_DIM // 2
    grid = (b, s // tm)
    in_specs = [
        pl.BlockSpec((1, tm, d), lambda i, j: (i, j, 0)),
        pl.BlockSpec((1, d), lambda i, j: (0, 0)),
        pl.BlockSpec((nt, d), lambda i, j: (0, 0)),
        pl.BlockSpec((d, n_nat), lambda i, j: (0, 0)),
        pl.BlockSpec((1, half, tm), lambda i, j: (i, 0, j)),
        pl.BlockSpec((1, half, tm), lambda i, j: (i, 0, j)),
        pl.BlockSpec((HEAD_DIM, 1), lambda i, j: (0, 0)),
        pl.BlockSpec((HEAD_DIM, 1), lambda i, j: (0, 0)),
    ]
    out_shape = [
        jax.ShapeDtypeStruct((b, nqb, BLK, Q_HEADS * BLK), BF16),
        jax.ShapeDtypeStruct((b, s, KV_HEADS * HEAD_DIM), BF16),
        jax.ShapeDtypeStruct((b, KV_HEADS * HEAD_DIM, s), BF16),
    ]
    out_specs = [
        pl.BlockSpec((1, tm // BLK, BLK, Q_HEADS * BLK), lambda i, j: (i, j, 0, 0)),
        pl.BlockSpec((1, tm, KV_HEADS * HEAD_DIM), lambda i, j: (i, j, 0)),
        pl.BlockSpec((1, KV_HEADS * HEAD_DIM, tm), lambda i, j: (i, 0, j)),
    ]
    if with_indexer:
        out_shape += [
            jax.ShapeDtypeStruct((b, nqb, BLK, IDX_HEADS * BLK), BF16),
            jax.ShapeDtypeStruct((b, s, BLK), BF16),
            jax.ShapeDtypeStruct((b, IDX_HEADS, s), F32),
        ]
        out_specs += [
            pl.BlockSpec((1, tm // BLK, BLK, IDX_HEADS * BLK), lambda i, j: (i, j, 0, 0)),
            pl.BlockSpec((1, tm, BLK), lambda i, j: (i, j, 0)),
            pl.BlockSpec((1, IDX_HEADS, tm), lambda i, j: (i, 0, j)),
        ]
    out_shape.append(jax.ShapeDtypeStruct((b, s, n_nat), F32))
    out_specs.append(pl.BlockSpec((1, tm, n_nat), lambda i, j: (i, j, 0)))
    return pl.pallas_call(
        functools.partial(_proj_kernel, tm=tm, with_indexer=with_indexer, n_nat=n_nat),
        grid=grid,
        in_specs=in_specs,
        out_specs=out_specs,
        out_shape=out_shape,
        compiler_params=_cparams(("arbitrary", "arbitrary")),
        name="in_proj_dsa" if with_indexer else "in_proj_swa",
    )(x, gain.reshape(1, d), w_t, w_n, cos_t, sin_t, q_gain.reshape(HEAD_DIM, 1), k_gain.reshape(HEAD_DIM, 1))


PAIR = 2 * BLK
QUAD = 4 * BLK
COUNT_ROWS = 64
FLT_LOWEST = float(np.finfo(np.float32).min)
SEARCH_FIRST_STEPS = 12
SEARCH_STEPS_PER_CHECK = 4
SEARCH_MAX_STEPS = 20


def _fold8(x):
    return x.reshape(x.shape[0] // 8, 8, x.shape[1])


def _pair_loop(npair, pair_fn, carry):
    start = 0
    for width in (4, 2, 1):
        trips = (npair - start) // width

        def body(i, c, start=start, width=width):
            for u in range(width):
                c = pair_fn(start + width * i + u, c)
            return c
        carry = lax.fori_loop(0, trips, body, carry)
        start = start + width * trips
    return carry


def _count(sc_ref, nquad, thr):
    def body(i, acc):
        off = pl.multiple_of(i * QUAD, QUAD)
        hit = jnp.where(sc_ref[pl.ds(off, QUAD), :] >= thr, 1.0, 0.0)
        return acc + jnp.sum(hit.reshape(QUAD // COUNT_ROWS, COUNT_ROWS, BLK), axis=0)
    acc = lax.fori_loop(0, nquad, body, jnp.zeros((COUNT_ROWS, BLK), F32))
    return jnp.sum(acc, axis=0, keepdims=True)


def _resolve_surplus(sc_ref, nquad, thr, excess):
    def smallest_selected():
        def min_body(i, acc):
            off = pl.multiple_of(i * QUAD, QUAD)
            blk = sc_ref[pl.ds(off, QUAD), :]
            return jnp.minimum(acc, jnp.min(_fold8(jnp.where(blk >= thr, blk, jnp.inf)), axis=0))
        low = jnp.min(lax.fori_loop(0, nquad, min_body, jnp.full((8, BLK), jnp.inf, F32)), axis=0, keepdims=True)

        def cnt_body(i, acc):
            off = pl.multiple_of(i * QUAD, QUAD)
            hit = jnp.where(sc_ref[pl.ds(off, QUAD), :] == low, 1.0, 0.0)
            return acc + jnp.sum(hit.reshape(QUAD // COUNT_ROWS, COUNT_ROWS, BLK), axis=0)
        copies = jnp.sum(lax.fori_loop(0, nquad, cnt_body, jnp.zeros((COUNT_ROWS, BLK), F32)), axis=0, keepdims=True)
        return low, copies

    def cond(st):
        ex, _, copies = st
        return jnp.max(jnp.where(ex > copies, 1.0, 0.0)) > 0.0

    def body(st):
        ex, low, copies = st
        whole = ex > copies
        gone = jnp.where(whole, low, jnp.inf)

        def drop_body(i, carry):
            off = pl.multiple_of(i * QUAD, QUAD)
            blk = sc_ref[pl.ds(off, QUAD), :]
            sc_ref[pl.ds(off, QUAD), :] = jnp.where(blk == gone, -jnp.inf, blk)
            return carry
        lax.fori_loop(0, nquad, drop_body, 0)
        ex = jnp.where(whole, ex - copies, ex)
        low, copies = smallest_selected()
        return ex, low, copies

    low, copies = smallest_selected()
    ex, low, copies = lax.while_loop(cond, body, (excess, low, copies))
    surplus = ex > 0.0
    return jnp.where(surplus, low, jnp.inf), jnp.where(surplus, copies - ex, 0.0)


def _search_step(sc_ref, nquad, topk, st):
    lo, hi, c_lo, c_hi, done = st
    guess = lo + (hi - lo) * 0.5
    inside = (guess > lo) & (guess < hi)
    c = _count(sc_ref, nquad, guess)
    move = inside & (done == 0.0)
    up = move & (c >= topk)
    down = move & (c < topk)
    lo = jnp.where(up, guess, lo)
    c_lo = jnp.where(up, c, c_lo)
    hi = jnp.where(down, guess, hi)
    c_hi = jnp.where(down, c, c_hi)
    done = jnp.where(inside & (c_lo != topk), done, 1.0)
    return lo, hi, c_lo, c_hi, done


def _dsa_kernel(q_ref, k_ref, vt_ref, iq_ref, ik_ref, iw_ref, o_ref, sc_ref, lg_ref, acc_ref, mx_ref, tri_ref,
                *, topk):
    qi = pl.program_id(1)
    npair = (qi + 2) // 2
    iq = iq_ref[0, 0]
    w = iw_ref[0]
    nquad = (qi + 4) // 4
    row2 = lax.broadcasted_iota(jnp.int32, (PAIR, BLK), 0)
    col2 = lax.broadcasted_iota(jnp.int32, (PAIR, BLK), 1)

    def score_body(pi, carry):
        lo8, hi8 = carry
        off = pl.multiple_of(pi * PAIR, PAIR)
        ikb = ik_ref[0, pl.ds(off, PAIR), :]
        tot = jnp.zeros((PAIR, BLK), F32)
        for h in range(IDX_HEADS):
            sc = jnp.dot(ikb, iq_ref[0, 0, :, h * BLK:(h + 1) * BLK], preferred_element_type=F32)
            tot = tot + jnp.maximum(sc, 0.0) * w[h:h + 1, :]
        causal = (pi * PAIR + row2) <= (qi * BLK + col2)
        sc_ref[pl.ds(off, PAIR), :] = jnp.where(causal, tot, -jnp.inf)
        lo8 = jnp.minimum(lo8, jnp.min(_fold8(jnp.where(causal, tot, jnp.inf)), axis=0))
        hi8 = jnp.maximum(hi8, jnp.max(_fold8(jnp.where(causal, tot, -jnp.inf)), axis=0))
        return lo8, hi8

    lo8, hi8 = _pair_loop(npair, score_body,
                          (jnp.full((8, BLK), jnp.inf, F32), jnp.full((8, BLK), -jnp.inf, F32)))

    @pl.when(2 * npair < 4 * nquad)
    def _():
        sc_ref[pl.ds(pl.multiple_of(npair * PAIR, PAIR), PAIR), :] = jnp.full((PAIR, BLK), -jnp.inf, F32)

    smin = jnp.min(lo8, axis=0, keepdims=True)
    smax = jnp.max(hi8, axis=0, keepdims=True)
    n_valid = (qi * BLK + 1 + lax.broadcasted_iota(jnp.int32, (1, BLK), 1)).astype(F32)
    few = n_valid <= topk
    above = smax + (jnp.abs(smax) * 1e-6 + 1e-30)
    state = (smin, above, n_valid, jnp.zeros((1, BLK), F32), jnp.where(few, 1.0, 0.0))

    def search_cond(st):
        i, inner = st
        return (i < SEARCH_MAX_STEPS) & (jnp.min(inner[4]) == 0.0)

    def search_body(st):
        i, inner = st
        for _ in range(SEARCH_STEPS_PER_CHECK):
            inner = _search_step(sc_ref, nquad, topk, inner)
        return i + SEARCH_STEPS_PER_CHECK, inner

    _, state = lax.fori_loop(0, SEARCH_FIRST_STEPS // SEARCH_STEPS_PER_CHECK, lambda i, st: search_body(st),
                             (jnp.int32(0), state))
    _, (lo, _, c_lo, _, _) = lax.while_loop(search_cond, search_body, (jnp.int32(SEARCH_FIRST_STEPS), state))
    thr = jnp.where(few, FLT_LOWEST, lo)
    excess = jnp.where(few, 0.0, c_lo - topk)

    def logits_of_pair(off, bias, mx):
        kb = k_ref[0, pl.ds(off, PAIR), :]
        parts = []
        for h in range(Q_HEADS):
            lgh = jnp.dot(kb, q_ref[0, 0, :, h * BLK:(h + 1) * BLK], preferred_element_type=F32) + bias
            lg_ref[pl.ds(off, PAIR), h * BLK:(h + 1) * BLK] = lgh
            parts.append(jnp.max(_fold8(lgh), axis=0))
        return jnp.maximum(mx, jnp.concatenate(parts, axis=1))

    mx0 = jnp.full((8, Q_HEADS * BLK), NEG_BIG, F32)
    has_surplus = jnp.max(excess) > 0.0

    @pl.when(jnp.logical_not(has_surplus))
    def _():
        def logit_body(pi, mx):
            off = pl.multiple_of(pi * PAIR, PAIR)
            return logits_of_pair(off, jnp.where(sc_ref[pl.ds(off, PAIR), :] >= thr, 0.0, NEG_BIG), mx)
        mx_ref[...] = _pair_loop(npair, logit_body, mx0)

    @pl.when(has_surplus)
    def _():
        low, keep = _resolve_surplus(sc_ref, nquad, thr, excess)
        row = lax.broadcasted_iota(jnp.int32, (PAIR, PAIR), 0)
        col = lax.broadcasted_iota(jnp.int32, (PAIR, PAIR), 1)
        tri_ref[...] = (col < row).astype(BF16)

        def logit_body(pi, carry):
            mx, seen = carry
            off = pl.multiple_of(pi * PAIR, PAIR)
            sc = sc_ref[pl.ds(off, PAIR), :]
            eq = sc == low
            eqf = jnp.where(eq, 1.0, 0.0)
            rank = jnp.dot(tri_ref[...], eqf.astype(BF16), preferred_element_type=F32) + seen
            bias = jnp.where(sc >= thr, 0.0, NEG_BIG)
            bias = jnp.where(eq & (rank >= keep), NEG_BIG, bias)
            return logits_of_pair(off, bias, mx), seen + jnp.sum(eqf, axis=0, keepdims=True)

        mx, _ = _pair_loop(npair, logit_body, (mx0, jnp.zeros((1, BLK), F32)))
        mx_ref[...] = mx

    m = jnp.max(mx_ref[...], axis=0, keepdims=True)

    acc_ref[...] = jnp.zeros(acc_ref.shape, F32)

    def pv_body(pi, ls):
        off = pl.multiple_of(pi * PAIR, PAIR)
        sums, pbs = [], []
        for h in range(Q_HEADS):
            cs = slice(h * BLK, (h + 1) * BLK)
            p = jnp.exp2(lg_ref[pl.ds(off, PAIR), cs] - m[:, cs])
            sums.append(jnp.sum(_fold8(p), axis=0))
            pbs.append(p.astype(BF16))
        for g in range(KV_HEADS):
            pb = jnp.concatenate(pbs[g * GROUP:(g + 1) * GROUP], axis=1)
            vt = vt_ref[0, g * HEAD_DIM:(g + 1) * HEAD_DIM, pl.ds(off, PAIR)]
            acc_ref[g] += jnp.dot(vt, pb, preferred_element_type=F32)
        return ls + jnp.concatenate(sums, axis=1)

    ls = _pair_loop(npair, pv_body, jnp.zeros((8, Q_HEADS * BLK), F32))
    l = jnp.sum(ls, axis=0, keepdims=True)

    for pair in range(Q_HEADS // 2):
        g, hh = divmod(2 * pair, GROUP)
        o = acc_ref[g] / l[:, g * GROUP * BLK:(g + 1) * GROUP * BLK]
        two = jnp.concatenate([o[:, hh * BLK:(hh + 1) * BLK], o[:, (hh + 1) * BLK:(hh + 2) * BLK]], axis=0)
        o_ref[0, :, pair * BLK:(pair + 1) * BLK] = two.T.astype(o_ref.dtype)


def dsa_attention(qg, k, vt, iq, ik, iw):
    b, s, _ = k.shape
    assert s % QUAD == 0, "the counting loops pad the score tile to whole groups of four key blocks"
    nqb = s // BLK
    topk = min(DSA_TOPK, s // 4)
    return pl.pallas_call(
        functools.partial(_dsa_kernel, topk=topk),
        grid=(b, nqb),
        in_specs=[
            pl.BlockSpec((1, 1, BLK, Q_HEADS * BLK), lambda i, j: (i, j, 0, 0)),
            pl.BlockSpec((1, s, KV_HEADS * HEAD_DIM), lambda i, j: (i, 0, 0)),
            pl.BlockSpec((1, KV_HEADS * HEAD_DIM, s), lambda i, j: (i, 0, 0)),
            pl.BlockSpec((1, 1, BLK, IDX_HEADS * BLK), lambda i, j: (i, j, 0, 0)),
            pl.BlockSpec((1, s, BLK), lambda i, j: (i, 0, 0)),
            pl.BlockSpec((1, IDX_HEADS, BLK), lambda i, j: (i, 0, j)),
        ],
        out_specs=pl.BlockSpec((1, BLK, Q_HEADS * HEAD_DIM), lambda i, j: (i, j, 0)),
        out_shape=jax.ShapeDtypeStruct((b, s, Q_HEADS * HEAD_DIM), BF16),
        scratch_shapes=[
            pltpu.VMEM((s, BLK), F32),
            pltpu.VMEM((s, Q_HEADS * BLK), F32),
            pltpu.VMEM((KV_HEADS, HEAD_DIM, GROUP * BLK), F32),
            pltpu.VMEM((8, Q_HEADS * BLK), F32),
            pltpu.VMEM((PAIR, PAIR), BF16),
        ],
        compiler_params=_cparams(("arbitrary", "arbitrary")),
        name="dsa_attention",
    )(qg, k, vt, iq, ik, iw)


def _split3(x):
    hi = x.astype(BF16)
    r1 = x - hi.astype(F32)
    mid = r1.astype(BF16)
    lo = (r1 - mid.astype(F32)).astype(BF16)
    return hi, mid, lo


def _hgrn_chunk(qraw, flog, v, lb, state_t):
    c = B_CHUNK
    q = qraw * jax.nn.sigmoid(qraw)
    f = lb + (1.0 - lb) * jax.nn.sigmoid(flog)
    k = 1.0 - f
    logf = jnp.log(f)
    row = lax.broadcasted_iota(jnp.int32, (c, c), 0)
    lane = lax.broadcasted_iota(jnp.int32, (c, c), 1)
    incl = (lane <= row).astype(BF16)
    b = sum(jnp.dot(incl, part, preferred_element_type=F32) for part in _split3(logf))

    att = jnp.zeros((c, c), F32)
    sub_pos = row & (B_SUB - 1)
    kf = k
    for delta in range(B_SUB):
        if delta > 0:
            kf = pltpu.roll(kf, 1, axis=0) * f
        diag = jnp.sum(q * kf, axis=-1, keepdims=True)
        att = jnp.where((lane == row - delta) & (sub_pos >= delta), diag, att)

    rows = [jnp.zeros((B_SUB, c), F32)]
    for i in range(1, c // B_SUB):
        b_i = b[i * B_SUB - 1:i * B_SUB, :]
        kt = (k * jnp.exp(jnp.minimum(b_i - b, 0.0))).astype(BF16)
        qt = (q[i * B_SUB:(i + 1) * B_SUB] * jnp.exp(b[i * B_SUB:(i + 1) * B_SUB] - b_i)).astype(BF16)
        rows.append(lax.dot_general(qt, kt, (((1,), (1,)), ((), ())), preferred_element_type=F32))
    cross = jnp.concatenate(rows, axis=0)
    att = jnp.where((lane // B_SUB) < (row // B_SUB), cross, att)

    vb = v.astype(BF16)
    o = jnp.dot(att.astype(BF16), vb, preferred_element_type=F32)
    o = o + lax.dot_general((q * jnp.exp(b)).astype(BF16), state_t.astype(BF16), (((1,), (1,)), ((), ())),
                            preferred_element_type=F32)
    b_last = b[c - 1:c, :]
    kd = (k * jnp.exp(b_last - b)).astype(BF16)
    new_state_t = state_t * jnp.exp(b_last) + lax.dot_general(vb, kd, (((0,), (0,)), ((), ())),
                                                              preferred_element_type=F32)
    return o, new_state_t


def _hgrn_kernel(q_ref, f_ref, v_ref, g_ref, lbl_ref, gain_ref, o_ref, state_ref, *, layer, ts):
    @pl.when(pl.program_id(1) == 0)
    def _():
        state_ref[...] = jnp.zeros(state_ref.shape, F32)

    lbl = lbl_ref[...]
    e = jnp.exp(lbl - jnp.max(lbl, axis=0, keepdims=True))
    lb_all = jnp.sum(e[:layer + 1], axis=0, keepdims=True) / jnp.sum(e, axis=0, keepdims=True)
    gain = gain_ref[...]

    def chunk_body(ci, carry):
        r0 = pl.multiple_of(ci * B_CHUNK, B_CHUNK)
        for h in range(B_HEADS):
            cs = slice(h * B_HEAD_DIM, (h + 1) * B_HEAD_DIM)
            o, st = _hgrn_chunk(q_ref[0, pl.ds(r0, B_CHUNK), cs], f_ref[0, pl.ds(r0, B_CHUNK), cs],
                                v_ref[0, pl.ds(r0, B_CHUNK), cs], lb_all[:, cs], state_ref[h])
            state_ref[h] = st
            ms = jnp.mean(o * o, axis=-1, keepdims=True)
            o = (o * lax.rsqrt(ms + RMS_EPS)) * gain
            gate = g_ref[0, pl.ds(r0, B_CHUNK), cs]
            o_ref[0, pl.ds(r0, B_CHUNK), cs] = (o * (gate * jax.nn.sigmoid(gate))).astype(o_ref.dtype)
        return carry

    lax.fori_loop(0, ts // B_CHUNK, chunk_body, 0, unroll=4)


def hgrn2_mixer(zb, lb_logits, out_gain, layer, *, ts=256):
    b, s, _ = zb.shape
    width = B_HEADS * B_HEAD_DIM
    ts = min(ts, s)
    spec = lambda c: pl.BlockSpec((1, ts, width), lambda i, t, c=c: (i, t, c))
    return pl.pallas_call(
        functools.partial(_hgrn_kernel, layer=layer, ts=ts),
        grid=(b, s // ts),
        in_specs=[spec(0), spec(1), spec(2), spec(3),
                  pl.BlockSpec(lb_logits.shape, lambda i, t: (0, 0)),
                  pl.BlockSpec((1, B_HEAD_DIM), lambda i, t: (0, 0))],
        out_specs=pl.BlockSpec((1, ts, width), lambda i, t: (i, t, 0)),
        out_shape=jax.ShapeDtypeStruct((b, s, width), BF16),
        scratch_shapes=[pltpu.VMEM((B_HEADS, B_HEAD_DIM, B_HEAD_DIM), F32)],
        compiler_params=_cparams(("arbitrary", "arbitrary")),
        name="hgrn2_mixer",
    )(zb, zb, zb, zb, lb_logits.astype(F32), out_gain.reshape(1, B_HEAD_DIM).astype(F32))


def _shift_rows(x, prev8, k, row8):
    if k == 0:
        return x
    rolled = pltpu.roll(x, k, axis=0)
    head = jnp.where(row8 < k, pltpu.roll(prev8, k, axis=0), rolled[:8])
    return jnp.concatenate([head, rolled[8:]], axis=0)


def _rglru_kernel(g_ref, x_ref, cw_ref, cb_ref, wr_ref, br_ref, wi_ref, bi_ref, lam_ref, o_ref,
                  xprev_ref, hprev_ref, *, ts):
    @pl.when(pl.program_id(1) == 0)
    def _():
        xprev_ref[...] = jnp.zeros(xprev_ref.shape, F32)
        hprev_ref[...] = jnp.zeros(hprev_ref.shape, F32)

    x = x_ref[0]
    width = x.shape[1]
    prev8 = xprev_ref[...]
    row8 = lax.broadcasted_iota(jnp.int32, (8, width), 0)
    cw = cw_ref[...]
    xc = cb_ref[...] + sum(_shift_rows(x, prev8, k, row8) * cw[CONV_WIDTH - 1 - k:CONV_WIDTH - k, :]
                           for k in range(CONV_WIDTH))
    xprev_ref[...] = x[ts - 8:]

    xb = xc.astype(BF16)
    r = jax.nn.sigmoid(jnp.dot(xb, wr_ref[...], preferred_element_type=F32) + br_ref[...])
    ig = jax.nn.sigmoid(jnp.dot(xb, wi_ref[...], preferred_element_type=F32) + bi_ref[...])
    lam = lam_ref[...]
    softplus_neg = jnp.maximum(-lam, 0.0) + jnp.log(1.0 + jnp.exp(-jnp.abs(lam)))
    a = jnp.exp((-RG_C * r) * softplus_neg)
    u = jnp.sqrt(1.0 - a * a) * (ig * xc)

    pos = lax.broadcasted_iota(jnp.int32, (ts, width), 0) & 7
    d = 1
    while d < 8:
        keep = pos >= d
        u = jnp.where(keep, a * pltpu.roll(u, d, axis=0) + u, u)
        a = jnp.where(keep, a * pltpu.roll(a, d, axis=0), a)
        d *= 2
    carry = hprev_ref[0:1, :]
    hs = []
    for g in range(ts // 8):
        hg = u[8 * g:8 * g + 8] + a[8 * g:8 * g + 8] * carry
        carry = hg[7:8]
        hs.append(hg)
    h = jnp.concatenate(hs, axis=0)
    hprev_ref[...] = jnp.broadcast_to(carry, hprev_ref.shape)

    gt = g_ref[0]
    gelu = 0.5 * gt * (1.0 + jnp.tanh(np.float32(math.sqrt(2.0 / math.pi)) * (gt + 0.044715 * (gt * gt * gt))))
    o_ref[0] = (gelu * h).astype(o_ref.dtype)


def rglru_mixer(zn, conv_w, conv_b, wr_bd, br, wi_bd, bi, lam, *, ts=256):
    b, s, two_c = zn.shape
    c = two_c // 2
    ts = min(ts, s)
    vec = lambda a: a.reshape(1, c).astype(F32)
    const = lambda shape: pl.BlockSpec(shape, lambda i, t: (0, 0))
    return pl.pallas_call(
        functools.partial(_rglru_kernel, ts=ts),
        grid=(b, s // ts),
        in_specs=[pl.BlockSpec((1, ts, c), lambda i, t: (i, t, 0)),
                  pl.BlockSpec((1, ts, c), lambda i, t: (i, t, 1)),
                  const((CONV_WIDTH, c)), const((1, c)), const((c, c)), const((1, c)),
                  const((c, c)), const((1, c)), const((1, c))],
        out_specs=pl.BlockSpec((1, ts, c), lambda i, t: (i, t, 0)),
        out_shape=jax.ShapeDtypeStruct((b, s, c), BF16),
        scratch_shapes=[pltpu.VMEM((8, c), F32), pltpu.VMEM((8, c), F32)],
        compiler_params=_cparams(("arbitrary", "arbitrary")),
        name="rglru_mixer",
    )(zn, zn, conv_w.astype(F32), vec(conv_b), wr_bd, vec(br), wi_bd, vec(bi), vec(lam))


def _block_diag(w):
    n, c, d = w.shape
    eye = jnp.eye(n, dtype=w.dtype)
    return (eye[:, None, :, None] * w[:, :, None, :]).reshape(n * c, n * d)


SWA_BLOCKS_PER_STEP = 4


def _swa_kernel(q_ref, kp_ref, kc_ref, vp_ref, vc_ref, sink_ref, o_ref, *, nblk):
    j = pl.program_id(1)
    row = lax.broadcasted_iota(jnp.int32, (PAIR, BLK), 0)
    col = lax.broadcasted_iota(jnp.int32, (PAIR, BLK), 1)
    bias = jnp.where(row < BLK, jnp.where(row > col, 0.0, NEG_BIG), jnp.where(row - BLK <= col, 0.0, NEG_BIG))
    bias_first = jnp.where(j > 0, bias, jnp.where(row < BLK, NEG_BIG, bias))
    bias = jnp.concatenate([bias] * GROUP, axis=1)
    bias_first = jnp.concatenate([bias_first] * GROUP, axis=1)
    for u in range(nblk):
        outs = []
        for g in range(KV_HEADS):
            rows = slice(g * HEAD_DIM, (g + 1) * HEAD_DIM)
            if u == 0:
                keys = jnp.concatenate([kp_ref[0], kc_ref[0, 0:BLK, :]], axis=0)
                vals = jnp.concatenate([vp_ref[0, rows, :], vc_ref[0, rows, 0:BLK]], axis=1)
            else:
                keys = kc_ref[0, (u - 1) * BLK:(u + 1) * BLK, :]
                vals = vc_ref[0, rows, (u - 1) * BLK:(u + 1) * BLK]
            q = q_ref[0, u, :, g * GROUP * BLK:(g + 1) * GROUP * BLK]
            lg = jnp.dot(keys, q, preferred_element_type=F32) + (bias_first if u == 0 else bias)
            sink = sink_ref[g] * LOG2E
            m = jnp.maximum(jnp.max(lg, axis=0, keepdims=True), sink)
            p = jnp.exp2(lg - m)
            den = jnp.sum(p, axis=0, keepdims=True) + jnp.exp2(sink - m)
            outs.append(jnp.dot(vals, p.astype(BF16), preferred_element_type=F32) / den)
        for pair in range(Q_HEADS // 2):
            g, hh = divmod(2 * pair, GROUP)
            o = outs[g]
            two = jnp.concatenate([o[:, hh * BLK:(hh + 1) * BLK], o[:, (hh + 1) * BLK:(hh + 2) * BLK]], axis=0)
            o_ref[0, u * BLK:(u + 1) * BLK, pair * BLK:(pair + 1) * BLK] = two.T.astype(o_ref.dtype)


def swa_attention(q, k, vt, sinks):
    b, s, _ = k.shape
    nblk = SWA_BLOCKS_PER_STEP
    assert s % (nblk * BLK) == 0
    rows = nblk * BLK
    sink_rows = jnp.repeat(sinks.astype(F32), BLK).reshape(KV_HEADS, 1, GROUP * BLK)
    prev = lambda j: jnp.maximum(nblk * j - 1, 0)
    return pl.pallas_call(
        functools.partial(_swa_kernel, nblk=nblk),
        grid=(b, s // rows),
        in_specs=[
            pl.BlockSpec((1, nblk, BLK, Q_HEADS * BLK), lambda i, j: (i, j, 0, 0)),
            pl.BlockSpec((1, BLK, KV_HEADS * HEAD_DIM), lambda i, j: (i, prev(j), 0)),
            pl.BlockSpec((1, rows, KV_HEADS * HEAD_DIM), lambda i, j: (i, j, 0)),
            pl.BlockSpec((1, KV_HEADS * HEAD_DIM, BLK), lambda i, j: (i, 0, prev(j))),
            pl.BlockSpec((1, KV_HEADS * HEAD_DIM, rows), lambda i, j: (i, 0, j)),
            pl.BlockSpec((KV_HEADS, 1, GROUP * BLK), lambda i, j: (0, 0, 0)),
        ],
        out_specs=pl.BlockSpec((1, rows, Q_HEADS * HEAD_DIM), lambda i, j: (i, j, 0)),
        out_shape=jax.ShapeDtypeStruct((b, s, Q_HEADS * HEAD_DIM), BF16),
        compiler_params=_cparams(("arbitrary", "arbitrary")),
        name="swa_attention",
    )(q, k, k, vt, vt, sink_rows)


def _mlp_kernel(x_ref, oa_ref, ob_ref, wo_ref, g_ref, wu_ref, wd_ref, out_ref, *, hc):
    half = oa_ref.shape[1]
    x1 = (x_ref[...] + jnp.dot(oa_ref[...], wo_ref[:half, :], preferred_element_type=F32)
          + jnp.dot(ob_ref[...], wo_ref[half:, :], preferred_element_type=F32))
    ms = jnp.mean(x1 * x1, axis=-1, keepdims=True)
    hn = ((x1 * lax.rsqrt(ms + RMS_EPS)) * g_ref[...]).astype(BF16)
    out_ref[...] = x1
    hidden = wu_ref.shape[1]
    for c in range(hidden // hc):
        up = jnp.dot(hn, wu_ref[:, c * hc:(c + 1) * hc], preferred_element_type=F32)
        act = jnp.square(jnp.maximum(up, 0.0)).astype(BF16)
        out_ref[...] += jnp.dot(act, wd_ref[c * hc:(c + 1) * hc, :], preferred_element_type=F32)


def out_proj_mlp(x, oa, ob, w_out, gain, w_up, w_down, *, tm=512, hc=1024):
    b, s, d = x.shape
    t = b * s
    tm = min(tm, t)
    half = oa.shape[-1]
    hidden = w_up.shape[1]
    const = lambda shape: pl.BlockSpec(shape, lambda i: (0, 0), pipeline_mode=pl.Buffered(1))
    out = pl.pallas_call(
        functools.partial(_mlp_kernel, hc=hc),
        grid=(t // tm,),
        in_specs=[pl.BlockSpec((tm, d), lambda i: (i, 0)),
                  pl.BlockSpec((tm, half), lambda i: (i, 0)),
                  pl.BlockSpec((tm, half), lambda i: (i, 0)),
                  const((d, d)), const((1, d)), const((d, hidden)), const((hidden, d))],
        out_specs=pl.BlockSpec((tm, d), lambda i: (i, 0)),
        out_shape=jax.ShapeDtypeStruct((t, d), F32),
        compiler_params=_cparams(("arbitrary",)),
        name="out_proj_mlp",
    )(x.reshape(t, d), oa.reshape(t, half), ob.reshape(t, half), w_out.astype(BF16),
      gain.reshape(1, d).astype(F32), w_up.astype(BF16), w_down.astype(BF16))
    return out.reshape(b, s, d)


N_EVEN_T = Q_HEADS * HEAD_DIM + 2 * KV_HEADS * HEAD_DIM + IDX_HEADS * HEAD_DIM + HEAD_DIM + IDX_HEADS
N_ODD_NAT = 1024


def layer0_attention(x, pos, inp, j, rope=None):
    cos_t, sin_t = rope if rope is not None else rope_tables_t(pos)
    w_in = inp['even_w_in'][j]
    w_t = w_in[:, :N_EVEN_T].T.astype(BF16)
    w_n = w_in[:, N_EVEN_T:].astype(BF16)
    qg, k, vt, iq, ik, iw, zb = input_projection(
        x, inp['norm_mix_g'][2 * j], w_t, w_n, cos_t, sin_t, inp['a_q_norm_g'][j], inp['a_k_norm_g'][j],
        with_indexer=True)
    return dsa_attention(qg, k, vt, iq, ik, iw), zb


def layer1_mixers(x, pos, inp, j, rope=None):
    cos_t, sin_t = rope if rope is not None else rope_tables_t(pos)
    w_in = inp['odd_w_in'][j]
    w_n = w_in[:, :N_ODD_NAT].astype(BF16)
    w_t = w_in[:, N_ODD_NAT:].T.astype(BF16)
    qg, k, vt, zn = input_projection(
        x, inp['norm_mix_g'][2 * j + 1], w_t, w_n, cos_t, sin_t, inp['d_q_norm_g'][j], inp['d_k_norm_g'][j],
        with_indexer=False)
    o_c = rglru_mixer(zn, inp['c_conv_w'][j], inp['c_conv_b'][j],
                      _block_diag(inp['c_rgate_w'][j]).astype(BF16), inp['c_rgate_b'][j],
                      _block_diag(inp['c_igate_w'][j]).astype(BF16), inp['c_igate_b'][j], inp['c_lambda'][j])
    o_d = swa_attention(qg, k, vt, inp['d_sinks'][j])
    return o_c, o_d


def kernel(x, positions, norm_mix_g, norm_mlp_g, even_w_in, even_w_out, a_q_norm_g, a_k_norm_g, b_lb_logits,
           b_out_norm_g, odd_w_in, odd_w_out, c_conv_w, c_conv_b, c_rgate_w, c_rgate_b, c_igate_w, c_igate_b,
           c_lambda, d_q_norm_g, d_k_norm_g, d_sinks, mlp_w_up, mlp_w_down):
    inp = dict(norm_mix_g=norm_mix_g, even_w_in=even_w_in, a_q_norm_g=a_q_norm_g, a_k_norm_g=a_k_norm_g,
               odd_w_in=odd_w_in, c_conv_w=c_conv_w, c_conv_b=c_conv_b, c_rgate_w=c_rgate_w, c_rgate_b=c_rgate_b,
               c_igate_w=c_igate_w, c_igate_b=c_igate_b, c_lambda=c_lambda, d_q_norm_g=d_q_norm_g,
               d_k_norm_g=d_k_norm_g, d_sinks=d_sinks)
    depth = norm_mix_g.shape[0]
    rope = rope_tables_t(positions)
    for layer in range(depth):
        j = layer // 2
        if layer % 2 == 0:
            o_a, zb = layer0_attention(x, positions, inp, j, rope)
            o_b = hgrn2_mixer(zb, b_lb_logits, b_out_norm_g[j], j)
            x = out_proj_mlp(x, o_a, o_b, even_w_out[j], norm_mlp_g[layer], mlp_w_up[layer], mlp_w_down[layer])
        else:
            o_c, o_d = layer1_mixers(x, positions, inp, j, rope)
            x = out_proj_mlp(x, o_c, o_d, odd_w_out[j], norm_mlp_g[layer], mlp_w_up[layer], mlp_w_down[layer])
    return x
```

```python
import functools
import math

import numpy as np
import jax
import jax.numpy as jnp
from jax import lax
from jax.experimental import pallas as pl
from jax.experimental.pallas import tpu as pltpu

F32 = jnp.float32
BF16 = jnp.bfloat16

HEAD_DIM = 64
ROPE_THETA = 10000.0
RMS_EPS = 1e-6
KV_HEADS = 2
Q_HEADS = 8
GROUP = Q_HEADS // KV_HEADS
IDX_HEADS = 8
DSA_TOPK = 256
BLK = 128
B_HEAD_DIM = 128
B_HEADS = 4
B_CHUNK = 64
B_SUB = 16
CONV_WIDTH = 4
RG_C = 8.0
NEG_BIG = -1e30
LOG2E = math.log2(math.e)

VMEM_LIMIT = 56 * 1024 * 1024


def _cparams(sem):
    return pltpu.CompilerParams(dimension_semantics=sem, vmem_limit_bytes=VMEM_LIMIT)


def _rope_kernel(pos_ref, invf_ref, cos_ref, sin_ref):
    ang = pos_ref[0].astype(F32) * invf_ref[...]
    cos_ref[0] = jnp.cos(ang)
    sin_ref[0] = jnp.sin(ang)


def rope_tables_t(positions):
    b, s = positions.shape
    half = HEAD_DIM // 2
    invf = (1.0 / (ROPE_THETA ** (jnp.arange(0, HEAD_DIM, 2, dtype=F32) / HEAD_DIM))).reshape(half, 1)
    out = jax.ShapeDtypeStruct((b, half, s), F32)
    return pl.pallas_call(
        _rope_kernel,
        grid=(b,),
        in_specs=[pl.BlockSpec((1, 1, s), lambda i: (i, 0, 0)),
                  pl.BlockSpec((half, 1), lambda i: (0, 0))],
        out_specs=[pl.BlockSpec((1, half, s), lambda i: (i, 0, 0))] * 2,
        out_shape=[out, out],
        compiler_params=_cparams(("arbitrary",)),
        name="rope_tables",
    )(positions.reshape(b, 1, s), invf)


def _head_norm_rope_t(blk, gain_col, cos, sin, scale):
    if gain_col is not None:
        ms = jnp.mean(blk * blk, axis=0, keepdims=True)
        blk = (blk * lax.rsqrt(ms + RMS_EPS)) * gain_col
    half = HEAD_DIM // 2
    x1, x2 = blk[:half], blk[half:]
    o1 = x1 * cos - x2 * sin
    o2 = x2 * cos + x1 * sin
    out = jnp.concatenate([o1, o2], axis=0)
    if scale != 1.0:
        out = out * scale
    return out


def _proj_kernel(*refs, tm, with_indexer):
    if with_indexer:
        (x_ref, g_ref, wt_ref, wn_ref, cos_ref, sin_ref, qgain_ref, kgain_ref,
         qg_ref, k_ref, vt_ref, iq_ref, ik_ref, iw_ref, nat_ref) = refs
    else:
        (x_ref, g_ref, wt_ref, wn_ref, cos_ref, sin_ref, qgain_ref, kgain_ref,
         qg_ref, k_ref, vt_ref, nat_ref) = refs
    nq = tm // BLK
    x = x_ref[0]
    ms = jnp.mean(x * x, axis=-1, keepdims=True)
    hn = ((x * lax.rsqrt(ms + RMS_EPS)) * g_ref[...]).astype(BF16)
    zt = lax.dot_general(wt_ref[...], hn, (((1,), (1,)), ((), ())), preferred_element_type=F32)
    nat_ref[0] = jnp.dot(hn, wn_ref[...], preferred_element_type=F32)
    cos = cos_ref[0]
    sin = sin_ref[0]
    qgain = qgain_ref[...]
    kgain = kgain_ref[...]
    zeros64 = jnp.zeros((HEAD_DIM, BLK), BF16)
    r = 0
    for h in range(Q_HEADS):
        g, hh = divmod(h, GROUP)
        q = _head_norm_rope_t(zt[r:r + HEAD_DIM], qgain, cos, sin, HEAD_DIM ** -0.5 * LOG2E).astype(BF16)
        r += HEAD_DIM
        for j in range(nq):
            qg_ref[0, j, g * HEAD_DIM:(g + 1) * HEAD_DIM, h * BLK:(h + 1) * BLK] = q[:, j * BLK:(j + 1) * BLK]
            qg_ref[0, j, (1 - g) * HEAD_DIM:(2 - g) * HEAD_DIM, h * BLK:(h + 1) * BLK] = zeros64
    ks = []
    for g in range(KV_HEADS):
        ks.append(_head_norm_rope_t(zt[r:r + HEAD_DIM], kgain, cos, sin, 1.0))
        r += HEAD_DIM
    k_ref[0] = jnp.concatenate(ks, axis=0).T.astype(BF16)
    vt_ref[0] = zt[r:r + KV_HEADS * HEAD_DIM].astype(BF16)
    r += KV_HEADS * HEAD_DIM
    if with_indexer:
        for h in range(IDX_HEADS):
            iq = _head_norm_rope_t(zt[r:r + HEAD_DIM], None, cos, sin, 1.0).astype(BF16)
            r += HEAD_DIM
            for j in range(nq):
                iq_ref[0, j, 0:HEAD_DIM, h * BLK:(h + 1) * BLK] = iq[:, j * BLK:(j + 1) * BLK]
                iq_ref[0, j, HEAD_DIM:2 * HEAD_DIM, h * BLK:(h + 1) * BLK] = zeros64
        ikt = _head_norm_rope_t(zt[r:r + HEAD_DIM], None, cos, sin, 1.0)
        r += HEAD_DIM
        ik_ref[0] = jnp.concatenate([ikt, jnp.zeros_like(ikt)], axis=0).T.astype(BF16)
        iw_ref[0] = zt[r:r + IDX_HEADS] * (HEAD_DIM ** -0.5 * IDX_HEADS ** -0.5)
        r += IDX_HEADS


def input_projection(x, gain, w_t, w_n, cos_t, sin_t, q_gain, k_gain, *, with_indexer, tm=512):
    b, s, d = x.shape
    tm = min(tm, s)
    nt = w_t.shape[0]
    n_nat = w_n.shape[1]
    nqb = s // BLK
    half = HEAD_DIM // 2
    grid = (b, s // tm)
    in_specs = [
        pl.BlockSpec((1, tm, d), lambda i, j: (i, j, 0)),
        pl.BlockSpec((1, d), lambda i, j: (0, 0)),
        pl.BlockSpec((nt, d), lambda i, j: (0, 0)),
        pl.BlockSpec((d, n_nat), lambda i, j: (0, 0)),
        pl.BlockSpec((1, half, tm), lambda i, j: (i, 0, j)),
        pl.BlockSpec((1, half, tm), lambda i, j: (i, 0, j)),
        pl.BlockSpec((HEAD_DIM, 1), lambda i, j: (0, 0)),
        pl.BlockSpec((HEAD_DIM, 1), lambda i, j: (0, 0)),
    ]
    out_shape = [
        jax.ShapeDtypeStruct((b, nqb, BLK, Q_HEADS * BLK), BF16),
        jax.ShapeDtypeStruct((b, s, KV_HEADS * HEAD_DIM), BF16),
        jax.ShapeDtypeStruct((b, KV_HEADS * HEAD_DIM, s), BF16),
    ]
    out_specs = [
        pl.BlockSpec((1, tm // BLK, BLK, Q_HEADS * BLK), lambda i, j: (i, j, 0, 0)),
        pl.BlockSpec((1, tm, KV_HEADS * HEAD_DIM), lambda i, j: (i, j, 0)),
        pl.BlockSpec((1, KV_HEADS * HEAD_DIM, tm), lambda i, j: (i, 0, j)),
    ]
    if with_indexer:
        out_shape += [
            jax.ShapeDtypeStruct((b, nqb, BLK, IDX_HEADS * BLK), BF16),
            jax.ShapeDtypeStruct((b, s, BLK), BF16),
            jax.ShapeDtypeStruct((b, IDX_HEADS, s), F32),
        ]
        out_specs += [
            pl.BlockSpec((1, tm // BLK, BLK, IDX_HEADS * BLK), lambda i, j: (i, j, 0, 0)),
            pl.BlockSpec((1, tm, BLK), lambda i, j: (i, j, 0)),
            pl.BlockSpec((1, IDX_HEADS, tm), lambda i, j: (i, 0, j)),
        ]
    out_shape.append(jax.ShapeDtypeStruct((b, s, n_nat), F32))
    out_specs.append(pl.BlockSpec((1, tm, n_nat), lambda i, j: (i, j, 0)))
    return pl.pallas_call(
        functools.partial(_proj_kernel, tm=tm, with_indexer=with_indexer),
        grid=grid,
        in_specs=in_specs,
        out_specs=out_specs,
        out_shape=out_shape,
        compiler_params=_cparams(("arbitrary", "arbitrary")),
        name="in_proj_dsa" if with_indexer else "in_proj_swa",
    )(x, gain.reshape(1, d), w_t, w_n, cos_t, sin_t, q_gain.reshape(HEAD_DIM, 1), k_gain.reshape(HEAD_DIM, 1))


PAIR = 2 * BLK
QUAD = 4 * BLK
COUNT_ROWS = 64
FLT_LOWEST = float(np.finfo(np.float32).min)
SEARCH_STEPS = 16
SEARCH_STEPS_PER_TRIP = 4


def _fold8(x):
    return x.reshape(x.shape[0] // 8, 8, x.shape[1])


def _pair_loop(npair, pair_fn, carry):
    start = 0
    for width in (4, 2, 1):
        trips = (npair - start) // width

        def body(i, c, start=start, width=width):
            for u in range(width):
                c = pair_fn(start + width * i + u, c)
            return c
        carry = lax.fori_loop(0, trips, body, carry)
        start = start + width * trips
    return carry


def _count(sc_ref, nquad, thr):
    def body(i, acc):
        off = pl.multiple_of(i * QUAD, QUAD)
        hit = jnp.where(sc_ref[pl.ds(off, QUAD), :] >= thr, 1.0, 0.0)
        return acc + jnp.sum(hit.reshape(QUAD // COUNT_ROWS, COUNT_ROWS, BLK), axis=0)
    acc = lax.fori_loop(0, nquad, body, jnp.zeros((COUNT_ROWS, BLK), F32))
    return jnp.sum(acc, axis=0, keepdims=True)


def _resolve_surplus(sc_ref, nquad, thr, excess):
    def smallest_selected():
        def min_body(i, acc):
            off = pl.multiple_of(i * QUAD, QUAD)
            blk = sc_ref[pl.ds(off, QUAD), :]
            return jnp.minimum(acc, jnp.min(_fold8(jnp.where(blk >= thr, blk, jnp.inf)), axis=0))
        low = jnp.min(lax.fori_loop(0, nquad, min_body, jnp.full((8, BLK), jnp.inf, F32)), axis=0, keepdims=True)

        def cnt_body(i, acc):
            off = pl.multiple_of(i * QUAD, QUAD)
            hit = jnp.where(sc_ref[pl.ds(off, QUAD), :] == low, 1.0, 0.0)
            return acc + jnp.sum(hit.reshape(QUAD // COUNT_ROWS, COUNT_ROWS, BLK), axis=0)
        copies = jnp.sum(lax.fori_loop(0, nquad, cnt_body, jnp.zeros((COUNT_ROWS, BLK), F32)), axis=0, keepdims=True)
        return low, copies

    def cond(st):
        ex, _, copies = st
        return jnp.max(jnp.where(ex > copies, 1.0, 0.0)) > 0.0

    def body(st):
        ex, low, copies = st
        whole = ex > copies
        gone = jnp.where(whole, low, jnp.inf)

        def drop_body(i, carry):
            off = pl.multiple_of(i * QUAD, QUAD)
            blk = sc_ref[pl.ds(off, QUAD), :]
            sc_ref[pl.ds(off, QUAD), :] = jnp.where(blk == gone, -jnp.inf, blk)
            return carry
        lax.fori_loop(0, nquad, drop_body, 0)
        ex = jnp.where(whole, ex - copies, ex)
        low, copies = smallest_selected()
        return ex, low, copies

    low, copies = smallest_selected()
    ex, low, copies = lax.while_loop(cond, body, (excess, low, copies))
    surplus = ex > 0.0
    return jnp.where(surplus, low, jnp.inf), jnp.where(surplus, copies - ex, 0.0)


def _search_step(sc_ref, nquad, topk, st):
    lo, hi, c_lo, c_hi, done = st
    guess = lo + (hi - lo) * 0.5
    inside = (guess > lo) & (guess < hi)
    c = _count(sc_ref, nquad, guess)
    move = inside & (done == 0.0)
    up = move & (c >= topk)
    down = move & (c < topk)
    lo = jnp.where(up, guess, lo)
    c_lo = jnp.where(up, c, c_lo)
    hi = jnp.where(down, guess, hi)
    c_hi = jnp.where(down, c, c_hi)
    done = jnp.where(inside & (c_lo != topk), done, 1.0)
    return lo, hi, c_lo, c_hi, done


def _dsa_kernel(q_ref, k_ref, vt_ref, iq_ref, ik_ref, iw_ref, o_ref, sc_ref, lg_ref, acc_ref, mx_ref, tri_ref,
                tie_ref, *, topk):
    qi = pl.program_id(1)
    npair = (qi + 2) // 2
    iq = iq_ref[0, 0]
    w = iw_ref[0]
    nquad = (qi + 4) // 4
    row2 = lax.broadcasted_iota(jnp.int32, (PAIR, BLK), 0)
    col2 = lax.broadcasted_iota(jnp.int32, (PAIR, BLK), 1)

    def score_body(pi, carry):
        lo8, hi8 = carry
        off = pl.multiple_of(pi * PAIR, PAIR)
        ikb = ik_ref[0, pl.ds(off, PAIR), :]
        tot = jnp.zeros((PAIR, BLK), F32)
        for h in range(IDX_HEADS):
            sc = jnp.dot(ikb, iq_ref[0, 0, :, h * BLK:(h + 1) * BLK], preferred_element_type=F32)
            tot = tot + jnp.maximum(sc, 0.0) * w[h:h + 1, :]
        causal = (pi * PAIR + row2) <= (qi * BLK + col2)
        sc_ref[pl.ds(off, PAIR), :] = jnp.where(causal, tot, -jnp.inf)
        lo8 = jnp.minimum(lo8, jnp.min(_fold8(jnp.where(causal, tot, jnp.inf)), axis=0))
        hi8 = jnp.maximum(hi8, jnp.max(_fold8(jnp.where(causal, tot, -jnp.inf)), axis=0))
        return lo8, hi8

    lo8, hi8 = _pair_loop(npair, score_body,
                          (jnp.full((8, BLK), jnp.inf, F32), jnp.full((8, BLK), -jnp.inf, F32)))

    @pl.when(2 * npair < 4 * nquad)
    def _():
        sc_ref[pl.ds(pl.multiple_of(npair * PAIR, PAIR), PAIR), :] = jnp.full((PAIR, BLK), -jnp.inf, F32)

    smin = jnp.min(lo8, axis=0, keepdims=True)
    smax = jnp.max(hi8, axis=0, keepdims=True)
    n_valid = (qi * BLK + 1 + lax.broadcasted_iota(jnp.int32, (1, BLK), 1)).astype(F32)
    few = n_valid <= topk
    above = smax + (jnp.abs(smax) * 1e-6 + 1e-30)
    state = (smin, above, n_valid, jnp.zeros((1, BLK), F32), jnp.where(few, 1.0, 0.0))

    def search_body(i, st):
        for _ in range(SEARCH_STEPS_PER_TRIP):
            st = _search_step(sc_ref, nquad, topk, st)
        return st

    lo, _, c_lo, _, _ = lax.fori_loop(0, SEARCH_STEPS // SEARCH_STEPS_PER_TRIP, search_body, state)
    thr = jnp.where(few, FLT_LOWEST, lo)
    excess = jnp.where(few, 0.0, c_lo - topk)

    def logits_of_pair(off, bias, mx):
        kb = k_ref[0, pl.ds(off, PAIR), :]
        parts = []
        for h in range(Q_HEADS):
            lgh = jnp.dot(kb, q_ref[0, 0, :, h * BLK:(h + 1) * BLK], preferred_element_type=F32) + bias
            lg_ref[pl.ds(off, PAIR), h * BLK:(h + 1) * BLK] = lgh
            parts.append(jnp.max(_fold8(lgh), axis=0))
        return jnp.maximum(mx, jnp.concatenate(parts, axis=1))

    mx0 = jnp.full((8, Q_HEADS * BLK), NEG_BIG, F32)

    tie_ref[0:1, :] = jnp.full((1, BLK), jnp.inf, F32)
    tie_ref[1:2, :] = jnp.zeros((1, BLK), F32)

    @pl.when(jnp.max(excess) > 0.0)
    def _():
        low, keep = _resolve_surplus(sc_ref, nquad, thr, excess)
        tie_ref[0:1, :] = low
        tie_ref[1:2, :] = keep

    low = tie_ref[0:1, :]
    keep = tie_ref[1:2, :]
    need_rank = jnp.max(keep) > 0.0

    @pl.when(jnp.logical_not(need_rank))
    def _():
        def logit_body(pi, mx):
            off = pl.multiple_of(pi * PAIR, PAIR)
            sc = sc_ref[pl.ds(off, PAIR), :]
            return logits_of_pair(off, jnp.where((sc >= thr) & (sc != low), 0.0, NEG_BIG), mx)
        mx_ref[...] = _pair_loop(npair, logit_body, mx0)

    @pl.when(need_rank)
    def _():
        row = lax.broadcasted_iota(jnp.int32, (PAIR, PAIR), 0)
        col = lax.broadcasted_iota(jnp.int32, (PAIR, PAIR), 1)
        tri_ref[...] = (col < row).astype(BF16)

        def logit_body(pi, carry):
            mx, seen = carry
            off = pl.multiple_of(pi * PAIR, PAIR)
            sc = sc_ref[pl.ds(off, PAIR), :]
            eq = sc == low
            eqf = jnp.where(eq, 1.0, 0.0)
            rank = jnp.dot(tri_ref[...], eqf.astype(BF16), preferred_element_type=F32) + seen
            bias = jnp.where(sc >= thr, 0.0, NEG_BIG)
            bias = jnp.where(eq & (rank >= keep), NEG_BIG, bias)
            return logits_of_pair(off, bias, mx), seen + jnp.sum(eqf, axis=0, keepdims=True)

        mx, _ = _pair_loop(npair, logit_body, (mx0, jnp.zeros((1, BLK), F32)))
        mx_ref[...] = mx

    m = jnp.max(mx_ref[...], axis=0, keepdims=True)

    acc_ref[...] = jnp.zeros(acc_ref.shape, F32)

    def pv_body(pi, ls):
        off = pl.multiple_of(pi * PAIR, PAIR)
        sums, pbs = [], []
        for h in range(Q_HEADS):
            cs = slice(h * BLK, (h + 1) * BLK)
            p = jnp.exp2(lg_ref[pl.ds(off, PAIR), cs] - m[:, cs])
            sums.append(jnp.sum(_fold8(p), axis=0))
            pbs.append(p.astype(BF16))
        for g in range(KV_HEADS):
            pb = jnp.concatenate(pbs[g * GROUP:(g + 1) * GROUP], axis=1)
            vt = vt_ref[0, g * HEAD_DIM:(g + 1) * HEAD_DIM, pl.ds(off, PAIR)]
            acc_ref[g] += jnp.dot(vt, pb, preferred_element_type=F32)
        return ls + jnp.concatenate(sums, axis=1)

    ls = _pair_loop(npair, pv_body, jnp.zeros((8, Q_HEADS * BLK), F32))
    l = jnp.sum(ls, axis=0, keepdims=True)

    for pair in range(Q_HEADS // 2):
        g, hh = divmod(2 * pair, GROUP)
        o = acc_ref[g] / l[:, g * GROUP * BLK:(g + 1) * GROUP * BLK]
        two = jnp.concatenate([o[:, hh * BLK:(hh + 1) * BLK], o[:, (hh + 1) * BLK:(hh + 2) * BLK]], axis=0)
        o_ref[0, :, pair * BLK:(pair + 1) * BLK] = two.T.astype(o_ref.dtype)


def dsa_attention(qg, k, vt, iq, ik, iw):
    b, s, _ = k.shape
    assert s % QUAD == 0, "the counting loops pad the score tile to whole groups of four key blocks"
    nqb = s // BLK
    topk = min(DSA_TOPK, s // 4)
    return pl.pallas_call(
        functools.partial(_dsa_kernel, topk=topk),
        grid=(b, nqb),
        in_specs=[
            pl.BlockSpec((1, 1, BLK, Q_HEADS * BLK), lambda i, j: (i, j, 0, 0)),
            pl.BlockSpec((1, s, KV_HEADS * HEAD_DIM), lambda i, j: (i, 0, 0)),
            pl.BlockSpec((1, KV_HEADS * HEAD_DIM, s), lambda i, j: (i, 0, 0)),
            pl.BlockSpec((1, 1, BLK, IDX_HEADS * BLK), lambda i, j: (i, j, 0, 0)),
            pl.BlockSpec((1, s, BLK), lambda i, j: (i, 0, 0)),
            pl.BlockSpec((1, IDX_HEADS, BLK), lambda i, j: (i, 0, j)),
        ],
        out_specs=pl.BlockSpec((1, BLK, Q_HEADS * HEAD_DIM), lambda i, j: (i, j, 0)),
        out_shape=jax.ShapeDtypeStruct((b, s, Q_HEADS * HEAD_DIM), BF16),
        scratch_shapes=[
            pltpu.VMEM((s, BLK), F32),
            pltpu.VMEM((s, Q_HEADS * BLK), F32),
            pltpu.VMEM((KV_HEADS, HEAD_DIM, GROUP * BLK), F32),
            pltpu.VMEM((8, Q_HEADS * BLK), F32),
            pltpu.VMEM((PAIR, PAIR), BF16),
            pltpu.VMEM((8, BLK), F32),
        ],
        compiler_params=_cparams(("arbitrary", "arbitrary")),
        name="dsa_attention",
    )(qg, k, vt, iq, ik, iw)


def _split3(x):
    hi = x.astype(BF16)
    r1 = x - hi.astype(F32)
    mid = r1.astype(BF16)
    lo = (r1 - mid.astype(F32)).astype(BF16)
    return hi, mid, lo


def _hgrn_chunk(qraw, flog, v, lb, state_t):
    c = B_CHUNK
    q = qraw * jax.nn.sigmoid(qraw)
    f = lb + (1.0 - lb) * jax.nn.sigmoid(flog)
    k = 1.0 - f
    logf = jnp.log(f)
    row = lax.broadcasted_iota(jnp.int32, (c, c), 0)
    lane = lax.broadcasted_iota(jnp.int32, (c, c), 1)
    incl = (lane <= row).astype(BF16)
    b = sum(jnp.dot(incl, part, preferred_element_type=F32) for part in _split3(logf))

    att = jnp.zeros((c, c), F32)
    sub_pos = row & (B_SUB - 1)
    kf = k
    for delta in range(B_SUB):
        if delta > 0:
            kf = pltpu.roll(kf, 1, axis=0) * f
        diag = jnp.sum(q * kf, axis=-1, keepdims=True)
        att = jnp.where((lane == row - delta) & (sub_pos >= delta), diag, att)

    rows = [jnp.zeros((B_SUB, c), F32)]
    for i in range(1, c // B_SUB):
        b_i = b[i * B_SUB - 1:i * B_SUB, :]
        kt = (k * jnp.exp(jnp.minimum(b_i - b, 0.0))).astype(BF16)
        qt = (q[i * B_SUB:(i + 1) * B_SUB] * jnp.exp(b[i * B_SUB:(i + 1) * B_SUB] - b_i)).astype(BF16)
        rows.append(lax.dot_general(qt, kt, (((1,), (1,)), ((), ())), preferred_element_type=F32))
    cross = jnp.concatenate(rows, axis=0)
    att = jnp.where((lane // B_SUB) < (row // B_SUB), cross, att)

    vb = v.astype(BF16)
    o = jnp.dot(att.astype(BF16), vb, preferred_element_type=F32)
    o = o + lax.dot_general((q * jnp.exp(b)).astype(BF16), state_t.astype(BF16), (((1,), (1,)), ((), ())),
                            preferred_element_type=F32)
    b_last = b[c - 1:c, :]
    kd = (k * jnp.exp(b_last - b)).astype(BF16)
    new_state_t = state_t * jnp.exp(b_last) + lax.dot_general(vb, kd, (((0,), (0,)), ((), ())),
                                                              preferred_element_type=F32)
    return o, new_state_t


def _hgrn_kernel(q_ref, f_ref, v_ref, g_ref, lbl_ref, gain_ref, o_ref, state_ref, *, layer, ts):
    @pl.when(pl.program_id(1) == 0)
    def _():
        state_ref[...] = jnp.zeros(state_ref.shape, F32)

    lbl = lbl_ref[...]
    e = jnp.exp(lbl - jnp.max(lbl, axis=0, keepdims=True))
    lb_all = jnp.sum(e[:layer + 1], axis=0, keepdims=True) / jnp.sum(e, axis=0, keepdims=True)
    gain = gain_ref[...]

    def chunk_body(ci, carry):
        r0 = pl.multiple_of(ci * B_CHUNK, B_CHUNK)
        for h in range(B_HEADS):
            cs = slice(h * B_HEAD_DIM, (h + 1) * B_HEAD_DIM)
            o, st = _hgrn_chunk(q_ref[0, pl.ds(r0, B_CHUNK), cs], f_ref[0, pl.ds(r0, B_CHUNK), cs],
                                v_ref[0, pl.ds(r0, B_CHUNK), cs], lb_all[:, cs], state_ref[h])
            state_ref[h] = st
            ms = jnp.mean(o * o, axis=-1, keepdims=True)
            o = (o * lax.rsqrt(ms + RMS_EPS)) * gain
            gate = g_ref[0, pl.ds(r0, B_CHUNK), cs]
            o_ref[0, pl.ds(r0, B_CHUNK), cs] = (o * (gate * jax.nn.sigmoid(gate))).astype(o_ref.dtype)
        return carry

    lax.fori_loop(0, ts // B_CHUNK, chunk_body, 0, unroll=4)


def hgrn2_mixer(zb, lb_logits, out_gain, layer, *, ts=256):
    b, s, _ = zb.shape
    width = B_HEADS * B_HEAD_DIM
    ts = min(ts, s)
    spec = lambda c: pl.BlockSpec((1, ts, width), lambda i, t, c=c: (i, t, c))
    return pl.pallas_call(
        functools.partial(_hgrn_kernel, layer=layer, ts=ts),
        grid=(b, s // ts),
        in_specs=[spec(0), spec(1), spec(2), spec(3),
                  pl.BlockSpec(lb_logits.shape, lambda i, t: (0, 0)),
                  pl.BlockSpec((1, B_HEAD_DIM), lambda i, t: (0, 0))],
        out_specs=pl.BlockSpec((1, ts, width), lambda i, t: (i, t, 0)),
        out_shape=jax.ShapeDtypeStruct((b, s, width), BF16),
        scratch_shapes=[pltpu.VMEM((B_HEADS, B_HEAD_DIM, B_HEAD_DIM), F32)],
        compiler_params=_cparams(("arbitrary", "arbitrary")),
        name="hgrn2_mixer",
    )(zb, zb, zb, zb, lb_logits.astype(F32), out_gain.reshape(1, B_HEAD_DIM).astype(F32))


def _shift_rows(x, prev8, k, row8):
    if k == 0:
        return x
    rolled = pltpu.roll(x, k, axis=0)
    head = jnp.where(row8 < k, pltpu.roll(prev8, k, axis=0), rolled[:8])
    return jnp.concatenate([head, rolled[8:]], axis=0)


def _rglru_kernel(g_ref, x_ref, cw_ref, cb_ref, wr_ref, br_ref, wi_ref, bi_ref, lam_ref, o_ref,
                  xprev_ref, hprev_ref, *, ts):
    @pl.when(pl.program_id(1) == 0)
    def _():
        xprev_ref[...] = jnp.zeros(xprev_ref.shape, F32)
        hprev_ref[...] = jnp.zeros(hprev_ref.shape, F32)

    x = x_ref[0]
    width = x.shape[1]
    prev8 = xprev_ref[...]
    row8 = lax.broadcasted_iota(jnp.int32, (8, width), 0)
    cw = cw_ref[...]
    xc = cb_ref[...] + sum(_shift_rows(x, prev8, k, row8) * cw[CONV_WIDTH - 1 - k:CONV_WIDTH - k, :]
                           for k in range(CONV_WIDTH))
    xprev_ref[...] = x[ts - 8:]

    xb = xc.astype(BF16)
    r = jax.nn.sigmoid(jnp.dot(xb, wr_ref[...], preferred_element_type=F32) + br_ref[...])
    ig = jax.nn.sigmoid(jnp.dot(xb, wi_ref[...], preferred_element_type=F32) + bi_ref[...])
    lam = lam_ref[...]
    softplus_neg = jnp.maximum(-lam, 0.0) + jnp.log(1.0 + jnp.exp(-jnp.abs(lam)))
    a = jnp.exp((-RG_C * r) * softplus_neg)
    u = jnp.sqrt(1.0 - a * a) * (ig * xc)

    pos = lax.broadcasted_iota(jnp.int32, (ts, width), 0) & 7
    d = 1
    while d < 8:
        keep = pos >= d
        u = jnp.where(keep, a * pltpu.roll(u, d, axis=0) + u, u)
        a = jnp.where(keep, a * pltpu.roll(a, d, axis=0), a)
        d *= 2
    carry = hprev_ref[0:1, :]
    hs = []
    for g in range(ts // 8):
        hg = u[8 * g:8 * g + 8] + a[8 * g:8 * g + 8] * carry
        carry = hg[7:8]
        hs.append(hg)
    h = jnp.concatenate(hs, axis=0)
    hprev_ref[...] = jnp.broadcast_to(carry, hprev_ref.shape)

    gt = g_ref[0]
    gelu = 0.5 * gt * (1.0 + jnp.tanh(np.float32(math.sqrt(2.0 / math.pi)) * (gt + 0.044715 * (gt * gt * gt))))
    o_ref[0] = (gelu * h).astype(o_ref.dtype)


def rglru_mixer(zn, conv_w, conv_b, wr_bd, br, wi_bd, bi, lam, *, ts=256):
    b, s, two_c = zn.shape
    c = two_c // 2
    ts = min(ts, s)
    vec = lambda a: a.reshape(1, c).astype(F32)
    const = lambda shape: pl.BlockSpec(shape, lambda i, t: (0, 0))
    return pl.pallas_call(
        functools.partial(_rglru_kernel, ts=ts),
        grid=(b, s // ts),
        in_specs=[pl.BlockSpec((1, ts, c), lambda i, t: (i, t, 0)),
                  pl.BlockSpec((1, ts, c), lambda i, t: (i, t, 1)),
                  const((CONV_WIDTH, c)), const((1, c)), const((c, c)), const((1, c)),
                  const((c, c)), const((1, c)), const((1, c))],
        out_specs=pl.BlockSpec((1, ts, c), lambda i, t: (i, t, 0)),
        out_shape=jax.ShapeDtypeStruct((b, s, c), BF16),
        scratch_shapes=[pltpu.VMEM((8, c), F32), pltpu.VMEM((8, c), F32)],
        compiler_params=_cparams(("arbitrary", "arbitrary")),
        name="rglru_mixer",
    )(zn, zn, conv_w.astype(F32), vec(conv_b), wr_bd, vec(br), wi_bd, vec(bi), vec(lam))


def _block_diag(w):
    n, c, d = w.shape
    eye = jnp.eye(n, dtype=w.dtype)
    return (eye[:, None, :, None] * w[:, :, None, :]).reshape(n * c, n * d)


SWA_BLOCKS_PER_STEP = 4


def _swa_kernel(q_ref, kp_ref, kc_ref, vp_ref, vc_ref, sink_ref, o_ref, *, nblk):
    j = pl.program_id(1)
    row = lax.broadcasted_iota(jnp.int32, (PAIR, BLK), 0)
    col = lax.broadcasted_iota(jnp.int32, (PAIR, BLK), 1)
    bias = jnp.where(row < BLK, jnp.where(row > col, 0.0, NEG_BIG), jnp.where(row - BLK <= col, 0.0, NEG_BIG))
    bias_first = jnp.where(j > 0, bias, jnp.where(row < BLK, NEG_BIG, bias))
    bias = jnp.concatenate([bias] * GROUP, axis=1)
    bias_first = jnp.concatenate([bias_first] * GROUP, axis=1)
    for u in range(nblk):
        outs = []
        for g in range(KV_HEADS):
            rows = slice(g * HEAD_DIM, (g + 1) * HEAD_DIM)
            if u == 0:
                keys = jnp.concatenate([kp_ref[0], kc_ref[0, 0:BLK, :]], axis=0)
                vals = jnp.concatenate([vp_ref[0, rows, :], vc_ref[0, rows, 0:BLK]], axis=1)
            else:
                keys = kc_ref[0, (u - 1) * BLK:(u + 1) * BLK, :]
                vals = vc_ref[0, rows, (u - 1) * BLK:(u + 1) * BLK]
            q = q_ref[0, u, :, g * GROUP * BLK:(g + 1) * GROUP * BLK]
            lg = jnp.dot(keys, q, preferred_element_type=F32) + (bias_first if u == 0 else bias)
            sink = sink_ref[g] * LOG2E
            m = jnp.maximum(jnp.max(lg, axis=0, keepdims=True), sink)
            p = jnp.exp2(lg - m)
            den = jnp.sum(p, axis=0, keepdims=True) + jnp.exp2(sink - m)
            outs.append(jnp.dot(vals, p.astype(BF16), preferred_element_type=F32) / den)
        for pair in range(Q_HEADS // 2):
            g, hh = divmod(2 * pair, GROUP)
            o = outs[g]
            two = jnp.concatenate([o[:, hh * BLK:(hh + 1) * BLK], o[:, (hh + 1) * BLK:(hh + 2) * BLK]], axis=0)
            o_ref[0, u * BLK:(u + 1) * BLK, pair * BLK:(pair + 1) * BLK] = two.T.astype(o_ref.dtype)


def swa_attention(q, k, vt, sinks):
    b, s, _ = k.shape
    nblk = SWA_BLOCKS_PER_STEP
    assert s % (nblk * BLK) == 0
    rows = nblk * BLK
    sink_rows = jnp.repeat(sinks.astype(F32), BLK).reshape(KV_HEADS, 1, GROUP * BLK)
    prev = lambda j: jnp.maximum(nblk * j - 1, 0)
    return pl.pallas_call(
        functools.partial(_swa_kernel, nblk=nblk),
        grid=(b, s // rows),
        in_specs=[
            pl.BlockSpec((1, nblk, BLK, Q_HEADS * BLK), lambda i, j: (i, j, 0, 0)),
            pl.BlockSpec((1, BLK, KV_HEADS * HEAD_DIM), lambda i, j: (i, prev(j), 0)),
            pl.BlockSpec((1, rows, KV_HEADS * HEAD_DIM), lambda i, j: (i, j, 0)),
            pl.BlockSpec((1, KV_HEADS * HEAD_DIM, BLK), lambda i, j: (i, 0, prev(j))),
            pl.BlockSpec((1, KV_HEADS * HEAD_DIM, rows), lambda i, j: (i, 0, j)),
            pl.BlockSpec((KV_HEADS, 1, GROUP * BLK), lambda i, j: (0, 0, 0)),
        ],
        out_specs=pl.BlockSpec((1, rows, Q_HEADS * HEAD_DIM), lambda i, j: (i, j, 0)),
        out_shape=jax.ShapeDtypeStruct((b, s, Q_HEADS * HEAD_DIM), BF16),
        compiler_params=_cparams(("arbitrary", "arbitrary")),
        name="swa_attention",
    )(q, k, k, vt, vt, sink_rows)


def _mlp_kernel(x_ref, oa_ref, ob_ref, wo_ref, g_ref, wu_ref, wd_ref, out_ref, *, hc):
    half = oa_ref.shape[1]
    x1 = (x_ref[...] + jnp.dot(oa_ref[...], wo_ref[:half, :], preferred_element_type=F32)
          + jnp.dot(ob_ref[...], wo_ref[half:, :], preferred_element_type=F32))
    ms = jnp.mean(x1 * x1, axis=-1, keepdims=True)
    hn = ((x1 * lax.rsqrt(ms + RMS_EPS)) * g_ref[...]).astype(BF16)
    out_ref[...] = x1
    hidden = wu_ref.shape[1]
    for c in range(hidden // hc):
        up = jnp.dot(hn, wu_ref[:, c * hc:(c + 1) * hc], preferred_element_type=F32)
        act = jnp.square(jnp.maximum(up, 0.0)).astype(BF16)
        out_ref[...] += jnp.dot(act, wd_ref[c * hc:(c + 1) * hc, :], preferred_element_type=F32)


def out_proj_mlp(x, oa, ob, w_out, gain, w_up, w_down, *, tm=512, hc=1024):
    b, s, d = x.shape
    t = b * s
    tm = min(tm, t)
    half = oa.shape[-1]
    hidden = w_up.shape[1]
    const = lambda shape: pl.BlockSpec(shape, lambda i: (0, 0), pipeline_mode=pl.Buffered(1))
    out = pl.pallas_call(
        functools.partial(_mlp_kernel, hc=hc),
        grid=(t // tm,),
        in_specs=[pl.BlockSpec((tm, d), lambda i: (i, 0)),
                  pl.BlockSpec((tm, half), lambda i: (i, 0)),
                  pl.BlockSpec((tm, half), lambda i: (i, 0)),
                  const((d, d)), const((1, d)), const((d, hidden)), const((hidden, d))],
        out_specs=pl.BlockSpec((tm, d), lambda i: (i, 0)),
        out_shape=jax.ShapeDtypeStruct((t, d), F32),
        compiler_params=_cparams(("arbitrary",)),
        name="out_proj_mlp",
    )(x.reshape(t, d), oa.reshape(t, half), ob.reshape(t, half), w_out.astype(BF16),
      gain.reshape(1, d).astype(F32), w_up.astype(BF16), w_down.astype(BF16))
    return out.reshape(b, s, d)


N_EVEN_T = Q_HEADS * HEAD_DIM + 2 * KV_HEADS * HEAD_DIM + IDX_HEADS * HEAD_DIM + HEAD_DIM + IDX_HEADS
N_ODD_NAT = 2 * Q_HEADS * HEAD_DIM


def layer0_attention(x, pos, inp, j, rope=None):
    cos_t, sin_t = rope if rope is not None else rope_tables_t(pos)
    w_in = inp['even_w_in'][j]
    w_t = w_in[:, :N_EVEN_T].T.astype(BF16)
    w_n = w_in[:, N_EVEN_T:].astype(BF16)
    qg, k, vt, iq, ik, iw, zb = input_projection(
        x, inp['norm_mix_g'][2 * j], w_t, w_n, cos_t, sin_t, inp['a_q_norm_g'][j], inp['a_k_norm_g'][j],
        with_indexer=True)
    return dsa_attention(qg, k, vt, iq, ik, iw), zb


def layer1_mixers(x, pos, inp, j, rope=None):
    cos_t, sin_t = rope if rope is not None else rope_tables_t(pos)
    w_in = inp['odd_w_in'][j]
    w_n = w_in[:, :N_ODD_NAT].astype(BF16)
    w_t = w_in[:, N_ODD_NAT:].T.astype(BF16)
    qg, k, vt, zn = input_projection(
        x, inp['norm_mix_g'][2 * j + 1], w_t, w_n, cos_t, sin_t, inp['d_q_norm_g'][j], inp['d_k_norm_g'][j],
        with_indexer=False)
    o_c = rglru_mixer(zn, inp['c_conv_w'][j], inp['c_conv_b'][j],
                      _block_diag(inp['c_rgate_w'][j]).astype(BF16), inp['c_rgate_b'][j],
                      _block_diag(inp['c_igate_w'][j]).astype(BF16), inp['c_igate_b'][j], inp['c_lambda'][j])
    o_d = swa_attention(qg, k, vt, inp['d_sinks'][j])
    return o_c, o_d


def kernel(x, positions, norm_mix_g, norm_mlp_g, even_w_in, even_w_out, a_q_norm_g, a_k_norm_g, b_lb_logits,
           b_out_norm_g, odd_w_in, odd_w_out, c_conv_w, c_conv_b, c_rgate_w, c_rgate_b, c_igate_w, c_igate_b,
           c_lambda, d_q_norm_g, d_k_norm_g, d_sinks, mlp_w_up, mlp_w_down):
    inp = dict(norm_mix_g=norm_mix_g, even_w_in=even_w_in, a_q_norm_g=a_q_norm_g, a_k_norm_g=a_k_norm_g,
               odd_w_in=odd_w_in, c_conv_w=c_conv_w, c_conv_b=c_conv_b, c_rgate_w=c_rgate_w, c_rgate_b=c_rgate_b,
               c_igate_w=c_igate_w, c_igate_b=c_igate_b, c_lambda=c_lambda, d_q_norm_g=d_q_norm_g,
               d_k_norm_g=d_k_norm_g, d_sinks=d_sinks)
    depth = norm_mix_g.shape[0]
    rope = rope_tables_t(positions)
    for layer in range(depth):
        j = layer // 2
        if layer % 2 == 0:
            o_a, zb = layer0_attention(x, positions, inp, j, rope)
            o_b = hgrn2_mixer(zb, b_lb_logits, b_out_norm_g[j], j)
            x = out_proj_mlp(x, o_a, o_b, even_w_out[j], norm_mlp_g[layer], mlp_w_up[layer], mlp_w_down[layer])
        else:
            o_c, o_d = layer1_mixers(x, positions, inp, j, rope)
            x = out_proj_mlp(x, o_c, o_d, odd_w_out[j], norm_mlp_g[layer], mlp_w_up[layer], mlp_w_down[layer])
    return x
```

```python
import functools
import math

import numpy as np
import jax
import jax.numpy as jnp
from jax import lax
from jax.experimental import pallas as pl
from jax.experimental.pallas import tpu as pltpu

F32 = jnp.float32
BF16 = jnp.bfloat16

HEAD_DIM = 64
ROPE_THETA = 10000.0
RMS_EPS = 1e-6
KV_HEADS = 2
Q_HEADS = 8
GROUP = Q_HEADS // KV_HEADS
IDX_HEADS = 8
DSA_TOPK = 256
BLK = 128
B_HEAD_DIM = 128
B_HEADS = 4
B_CHUNK = 64
B_SUB = 16
CONV_WIDTH = 4
RG_C = 8.0
NEG_BIG = -1e30
LOG2E = math.log2(math.e)

VMEM_LIMIT = 56 * 1024 * 1024


def _cparams(sem):
    return pltpu.CompilerParams(dimension_semantics=sem, vmem_limit_bytes=VMEM_LIMIT)


def _rope_kernel(pos_ref, invf_ref, cos_ref, sin_ref):
    ang = pos_ref[0].astype(F32) * invf_ref[...]
    cos_ref[0] = jnp.cos(ang)
    sin_ref[0] = jnp.sin(ang)


def rope_tables_t(positions):
    b, s = positions.shape
    half = HEAD_DIM // 2
    invf = (1.0 / (ROPE_THETA ** (jnp.arange(0, HEAD_DIM, 2, dtype=F32) / HEAD_DIM))).reshape(half, 1)
    out = jax.ShapeDtypeStruct((b, half, s), F32)
    return pl.pallas_call(
        _rope_kernel,
        grid=(b,),
        in_specs=[pl.BlockSpec((1, 1, s), lambda i: (i, 0, 0)),
                  pl.BlockSpec((half, 1), lambda i: (0, 0))],
        out_specs=[pl.BlockSpec((1, half, s), lambda i: (i, 0, 0))] * 2,
        out_shape=[out, out],
        compiler_params=_cparams(("arbitrary",)),
        name="rope_tables",
    )(positions.reshape(b, 1, s), invf)


def _head_norm_rope_t(blk, gain_col, cos, sin, scale):
    if gain_col is not None:
        ms = jnp.mean(blk * blk, axis=0, keepdims=True)
        blk = (blk * lax.rsqrt(ms + RMS_EPS)) * gain_col
    half = HEAD_DIM // 2
    x1, x2 = blk[:half], blk[half:]
    o1 = x1 * cos - x2 * sin
    o2 = x2 * cos + x1 * sin
    out = jnp.concatenate([o1, o2], axis=0)
    if scale != 1.0:
        out = out * scale
    return out


def _proj_kernel(*refs, tm, with_indexer):
    if with_indexer:
        (x_ref, g_ref, wt_ref, wn_ref, cos_ref, sin_ref, qgain_ref, kgain_ref,
         qg_ref, k_ref, vt_ref, iq_ref, ik_ref, iw_ref, nat_ref) = refs
    else:
        (x_ref, g_ref, wt_ref, wn_ref, cos_ref, sin_ref, qgain_ref, kgain_ref,
         qg_ref, k_ref, vt_ref, nat_ref) = refs
    nq = tm // BLK
    x = x_ref[0]
    ms = jnp.mean(x * x, axis=-1, keepdims=True)
    hn = ((x * lax.rsqrt(ms + RMS_EPS)) * g_ref[...]).astype(BF16)
    zt = lax.dot_general(wt_ref[...], hn, (((1,), (1,)), ((), ())), preferred_element_type=F32)
    nat_ref[0] = jnp.dot(hn, wn_ref[...], preferred_element_type=F32)
    cos = cos_ref[0]
    sin = sin_ref[0]
    qgain = qgain_ref[...]
    kgain = kgain_ref[...]
    zeros64 = jnp.zeros((HEAD_DIM, BLK), BF16)
    r = 0
    for h in range(Q_HEADS):
        g, hh = divmod(h, GROUP)
        q = _head_norm_rope_t(zt[r:r + HEAD_DIM], qgain, cos, sin, HEAD_DIM ** -0.5 * LOG2E).astype(BF16)
        r += HEAD_DIM
        for j in range(nq):
            qg_ref[0, j, g * HEAD_DIM:(g + 1) * HEAD_DIM, h * BLK:(h + 1) * BLK] = q[:, j * BLK:(j + 1) * BLK]
            qg_ref[0, j, (1 - g) * HEAD_DIM:(2 - g) * HEAD_DIM, h * BLK:(h + 1) * BLK] = zeros64
    ks = []
    for g in range(KV_HEADS):
        ks.append(_head_norm_rope_t(zt[r:r + HEAD_DIM], kgain, cos, sin, 1.0))
        r += HEAD_DIM
    k_ref[0] = jnp.concatenate(ks, axis=0).T.astype(BF16)
    vt_ref[0] = zt[r:r + KV_HEADS * HEAD_DIM].astype(BF16)
    r += KV_HEADS * HEAD_DIM
    if with_indexer:
        for h in range(IDX_HEADS):
            iq = _head_norm_rope_t(zt[r:r + HEAD_DIM], None, cos, sin, 1.0).astype(BF16)
            r += HEAD_DIM
            for j in range(nq):
                iq_ref[0, j, 0:HEAD_DIM, h * BLK:(h + 1) * BLK] = iq[:, j * BLK:(j + 1) * BLK]
                iq_ref[0, j, HEAD_DIM:2 * HEAD_DIM, h * BLK:(h + 1) * BLK] = zeros64
        ikt = _head_norm_rope_t(zt[r:r + HEAD_DIM], None, cos, sin, 1.0)
        r += HEAD_DIM
        ik_ref[0] = jnp.concatenate([ikt, jnp.zeros_like(ikt)], axis=0).T.astype(BF16)
        iw_ref[0] = zt[r:r + IDX_HEADS] * (HEAD_DIM ** -0.5 * IDX_HEADS ** -0.5)
        r += IDX_HEADS


def input_projection(x, gain, w_t, w_n, cos_t, sin_t, q_gain, k_gain, *, with_indexer, tm=512):
    b, s, d = x.shape
    tm = min(tm, s)
    nt = w_t.shape[0]
    n_nat = w_n.shape[1]
    nqb = s // BLK
    half = HEAD_DIM // 2
    grid = (b, s // tm)
    in_specs = [
        pl.BlockSpec((1, tm, d), lambda i, j: (i, j, 0)),
        pl.BlockSpec((1, d), lambda i, j: (0, 0)),
        pl.BlockSpec((nt, d), lambda i, j: (0, 0)),
        pl.BlockSpec((d, n_nat), lambda i, j: (0, 0)),
        pl.BlockSpec((1, half, tm), lambda i, j: (i, 0, j)),
        pl.BlockSpec((1, half, tm), lambda i, j: (i, 0, j)),
        pl.BlockSpec((HEAD_DIM, 1), lambda i, j: (0, 0)),
        pl.BlockSpec((HEAD_DIM, 1), lambda i, j: (0, 0)),
    ]
    out_shape = [
        jax.ShapeDtypeStruct((b, nqb, BLK, Q_HEADS * BLK), BF16),
        jax.ShapeDtypeStruct((b, s, KV_HEADS * HEAD_DIM), BF16),
        jax.ShapeDtypeStruct((b, KV_HEADS * HEAD_DIM, s), BF16),
    ]
    out_specs = [
        pl.BlockSpec((1, tm // BLK, BLK, Q_HEADS * BLK), lambda i, j: (i, j, 0, 0)),
        pl.BlockSpec((1, tm, KV_HEADS * HEAD_DIM), lambda i, j: (i, j, 0)),
        pl.BlockSpec((1, KV_HEADS * HEAD_DIM, tm), lambda i, j: (i, 0, j)),
    ]
    if with_indexer:
        out_shape += [
            jax.ShapeDtypeStruct((b, nqb, BLK, IDX_HEADS * BLK), BF16),
            jax.ShapeDtypeStruct((b, s, BLK), BF16),
            jax.ShapeDtypeStruct((b, IDX_HEADS, s), F32),
        ]
        out_specs += [
            pl.BlockSpec((1, tm // BLK, BLK, IDX_HEADS * BLK), lambda i, j: (i, j, 0, 0)),
            pl.BlockSpec((1, tm, BLK), lambda i, j: (i, j, 0)),
            pl.BlockSpec((1, IDX_HEADS, tm), lambda i, j: (i, 0, j)),
        ]
    out_shape.append(jax.ShapeDtypeStruct((b, s, n_nat), F32))
    out_specs.append(pl.BlockSpec((1, tm, n_nat), lambda i, j: (i, j, 0)))
    return pl.pallas_call(
        functools.partial(_proj_kernel, tm=tm, with_indexer=with_indexer),
        grid=grid,
        in_specs=in_specs,
        out_specs=out_specs,
        out_shape=out_shape,
        compiler_params=_cparams(("arbitrary", "arbitrary")),
        name="in_proj_dsa" if with_indexer else "in_proj_swa",
    )(x, gain.reshape(1, d), w_t, w_n, cos_t, sin_t, q_gain.reshape(HEAD_DIM, 1), k_gain.reshape(HEAD_DIM, 1))


PAIR = 2 * BLK
QUAD = 4 * BLK
COUNT_ROWS = 64
PAIRS_PER_TRIP = (8, 4, 2, 1)
FLT_LOWEST = float(np.finfo(np.float32).min)
SEARCH_STEPS = 16
SEARCH_STEPS_PER_TRIP = 4


def _fold8(x):
    return x.reshape(x.shape[0] // 8, 8, x.shape[1])


def _pair_loop(npair, pair_fn, carry):
    start = 0
    for width in PAIRS_PER_TRIP:
        trips = (npair - start) // width

        def body(i, c, start=start, width=width):
            for u in range(width):
                c = pair_fn(start + width * i + u, c)
            return c
        carry = lax.fori_loop(0, trips, body, carry)
        start = start + width * trips
    return carry


def _count(sc_ref, nquad, thr):
    def body(i, acc):
        off = pl.multiple_of(i * QUAD, QUAD)
        hit = jnp.where(sc_ref[pl.ds(off, QUAD), :] >= thr, 1.0, 0.0)
        return acc + jnp.sum(hit.reshape(QUAD // COUNT_ROWS, COUNT_ROWS, BLK), axis=0)
    acc = lax.fori_loop(0, nquad, body, jnp.zeros((COUNT_ROWS, BLK), F32))
    return jnp.sum(acc, axis=0, keepdims=True)


def _resolve_surplus(sc_ref, nquad, thr, excess):
    def smallest_selected():
        def min_body(i, acc):
            off = pl.multiple_of(i * QUAD, QUAD)
            blk = sc_ref[pl.ds(off, QUAD), :]
            return jnp.minimum(acc, jnp.min(_fold8(jnp.where(blk >= thr, blk, jnp.inf)), axis=0))
        low = jnp.min(lax.fori_loop(0, nquad, min_body, jnp.full((8, BLK), jnp.inf, F32)), axis=0, keepdims=True)

        def cnt_body(i, acc):
            off = pl.multiple_of(i * QUAD, QUAD)
            hit = jnp.where(sc_ref[pl.ds(off, QUAD), :] == low, 1.0, 0.0)
            return acc + jnp.sum(hit.reshape(QUAD // COUNT_ROWS, COUNT_ROWS, BLK), axis=0)
        copies = jnp.sum(lax.fori_loop(0, nquad, cnt_body, jnp.zeros((COUNT_ROWS, BLK), F32)), axis=0, keepdims=True)
        return low, copies

    def cond(st):
        ex, _, copies = st
        return jnp.max(jnp.where(ex > copies, 1.0, 0.0)) > 0.0

    def body(st):
        ex, low, copies = st
        whole = ex > copies
        gone = jnp.where(whole, low, jnp.inf)

        def drop_body(i, carry):
            off = pl.multiple_of(i * QUAD, QUAD)
            blk = sc_ref[pl.ds(off, QUAD), :]
            sc_ref[pl.ds(off, QUAD), :] = jnp.where(blk == gone, -jnp.inf, blk)
            return carry
        lax.fori_loop(0, nquad, drop_body, 0)
        ex = jnp.where(whole, ex - copies, ex)
        low, copies = smallest_selected()
        return ex, low, copies

    low, copies = smallest_selected()
    ex, low, copies = lax.while_loop(cond, body, (excess, low, copies))
    surplus = ex > 0.0
    return jnp.where(surplus, low, jnp.inf), jnp.where(surplus, copies - ex, 0.0)


def _search_step(sc_ref, nquad, topk, st):
    lo, hi, c_lo, c_hi, done = st
    guess = lo + (hi - lo) * 0.5
    inside = (guess > lo) & (guess < hi)
    c = _count(sc_ref, nquad, guess)
    move = inside & (done == 0.0)
    up = move & (c >= topk)
    down = move & (c < topk)
    lo = jnp.where(up, guess, lo)
    c_lo = jnp.where(up, c, c_lo)
    hi = jnp.where(down, guess, hi)
    c_hi = jnp.where(down, c, c_hi)
    done = jnp.where(inside & (c_lo != topk), done, 1.0)
    return lo, hi, c_lo, c_hi, done


def _dsa_kernel(q_ref, k_ref, vt_ref, iq_ref, ik_ref, iw_ref, o_ref, sc_ref, lg_ref, acc_ref, mx_ref, tri_ref,
                tie_ref, *, topk):
    qi = pl.program_id(1)
    npair = (qi + 2) // 2
    iq = iq_ref[0, 0]
    w = iw_ref[0]
    nquad = (qi + 4) // 4
    row2 = lax.broadcasted_iota(jnp.int32, (PAIR, BLK), 0)
    col2 = lax.broadcasted_iota(jnp.int32, (PAIR, BLK), 1)

    def score_body(pi, carry):
        lo8, hi8 = carry
        off = pl.multiple_of(pi * PAIR, PAIR)
        ikb = ik_ref[0, pl.ds(off, PAIR), :]
        tot = jnp.zeros((PAIR, BLK), F32)
        for h in range(IDX_HEADS):
            sc = jnp.dot(ikb, iq_ref[0, 0, :, h * BLK:(h + 1) * BLK], preferred_element_type=F32)
            tot = tot + jnp.maximum(sc, 0.0) * w[h:h + 1, :]
        causal = (pi * PAIR + row2) <= (qi * BLK + col2)
        sc_ref[pl.ds(off, PAIR), :] = jnp.where(causal, tot, -jnp.inf)
        lo8 = jnp.minimum(lo8, jnp.min(_fold8(jnp.where(causal, tot, jnp.inf)), axis=0))
        hi8 = jnp.maximum(hi8, jnp.max(_fold8(jnp.where(causal, tot, -jnp.inf)), axis=0))
        return lo8, hi8

    lo8, hi8 = _pair_loop(npair, score_body,
                          (jnp.full((8, BLK), jnp.inf, F32), jnp.full((8, BLK), -jnp.inf, F32)))

    @pl.when(2 * npair < 4 * nquad)
    def _():
        sc_ref[pl.ds(pl.multiple_of(npair * PAIR, PAIR), PAIR), :] = jnp.full((PAIR, BLK), -jnp.inf, F32)

    smin = jnp.min(lo8, axis=0, keepdims=True)
    smax = jnp.max(hi8, axis=0, keepdims=True)
    n_valid = (qi * BLK + 1 + lax.broadcasted_iota(jnp.int32, (1, BLK), 1)).astype(F32)
    few = n_valid <= topk
    above = smax + (jnp.abs(smax) * 1e-6 + 1e-30)
    state = (smin, above, n_valid, jnp.zeros((1, BLK), F32), jnp.where(few, 1.0, 0.0))

    def search_body(i, st):
        for _ in range(SEARCH_STEPS_PER_TRIP):
            st = _search_step(sc_ref, nquad, topk, st)
        return st

    lo, _, c_lo, _, _ = lax.fori_loop(0, SEARCH_STEPS // SEARCH_STEPS_PER_TRIP, search_body, state)
    thr = jnp.where(few, FLT_LOWEST, lo)
    excess = jnp.where(few, 0.0, c_lo - topk)

    def logits_of_pair(off, bias, mx):
        kb = k_ref[0, pl.ds(off, PAIR), :]
        parts = []
        for h in range(Q_HEADS):
            lgh = jnp.dot(kb, q_ref[0, 0, :, h * BLK:(h + 1) * BLK], preferred_element_type=F32) + bias
            lg_ref[pl.ds(off, PAIR), h * BLK:(h + 1) * BLK] = lgh
            parts.append(jnp.max(_fold8(lgh), axis=0))
        return jnp.maximum(mx, jnp.concatenate(parts, axis=1))

    mx0 = jnp.full((8, Q_HEADS * BLK), NEG_BIG, F32)

    tie_ref[0:1, :] = jnp.full((1, BLK), jnp.inf, F32)
    tie_ref[1:2, :] = jnp.zeros((1, BLK), F32)

    @pl.when(jnp.max(excess) > 0.0)
    def _():
        low, keep = _resolve_surplus(sc_ref, nquad, thr, excess)
        tie_ref[0:1, :] = low
        tie_ref[1:2, :] = keep

    low = tie_ref[0:1, :]
    keep = tie_ref[1:2, :]
    need_rank = jnp.max(keep) > 0.0

    @pl.when(jnp.logical_not(need_rank))
    def _():
        def logit_body(pi, mx):
            off = pl.multiple_of(pi * PAIR, PAIR)
            sc = sc_ref[pl.ds(off, PAIR), :]
            return logits_of_pair(off, jnp.where((sc >= thr) & (sc != low), 0.0, NEG_BIG), mx)
        mx_ref[...] = _pair_loop(npair, logit_body, mx0)

    @pl.when(need_rank)
    def _():
        row = lax.broadcasted_iota(jnp.int32, (PAIR, PAIR), 0)
        col = lax.broadcasted_iota(jnp.int32, (PAIR, PAIR), 1)
        tri_ref[...] = (col < row).astype(BF16)

        def logit_body(pi, carry):
            mx, seen = carry
            off = pl.multiple_of(pi * PAIR, PAIR)
            sc = sc_ref[pl.ds(off, PAIR), :]
            eq = sc == low
            eqf = jnp.where(eq, 1.0, 0.0)
            rank = jnp.dot(tri_ref[...], eqf.astype(BF16), preferred_element_type=F32) + seen
            bias = jnp.where(sc >= thr, 0.0, NEG_BIG)
            bias = jnp.where(eq & (rank >= keep), NEG_BIG, bias)
            return logits_of_pair(off, bias, mx), seen + jnp.sum(eqf, axis=0, keepdims=True)

        mx, _ = _pair_loop(npair, logit_body, (mx0, jnp.zeros((1, BLK), F32)))
        mx_ref[...] = mx

    m = jnp.max(mx_ref[...], axis=0, keepdims=True)

    acc_ref[...] = jnp.zeros(acc_ref.shape, F32)

    def pv_body(pi, ls):
        off = pl.multiple_of(pi * PAIR, PAIR)
        sums, pbs = [], []
        for h in range(Q_HEADS):
            cs = slice(h * BLK, (h + 1) * BLK)
            p = jnp.exp2(lg_ref[pl.ds(off, PAIR), cs] - m[:, cs])
            sums.append(jnp.sum(_fold8(p), axis=0))
            pbs.append(p.astype(BF16))
        for g in range(KV_HEADS):
            pb = jnp.concatenate(pbs[g * GROUP:(g + 1) * GROUP], axis=1)
            vt = vt_ref[0, g * HEAD_DIM:(g + 1) * HEAD_DIM, pl.ds(off, PAIR)]
            acc_ref[g] += jnp.dot(vt, pb, preferred_element_type=F32)
        return ls + jnp.concatenate(sums, axis=1)

    ls = _pair_loop(npair, pv_body, jnp.zeros((8, Q_HEADS * BLK), F32))
    l = jnp.sum(ls, axis=0, keepdims=True)

    for pair in range(Q_HEADS // 2):
        g, hh = divmod(2 * pair, GROUP)
        o = acc_ref[g] / l[:, g * GROUP * BLK:(g + 1) * GROUP * BLK]
        two = jnp.concatenate([o[:, hh * BLK:(hh + 1) * BLK], o[:, (hh + 1) * BLK:(hh + 2) * BLK]], axis=0)
        o_ref[0, :, pair * BLK:(pair + 1) * BLK] = two.T.astype(o_ref.dtype)


def dsa_attention(qg, k, vt, iq, ik, iw):
    b, s, _ = k.shape
    assert s % QUAD == 0, "the counting loops pad the score tile to whole groups of four key blocks"
    nqb = s // BLK
    topk = min(DSA_TOPK, s // 4)
    return pl.pallas_call(
        functools.partial(_dsa_kernel, topk=topk),
        grid=(b, nqb),
        in_specs=[
            pl.BlockSpec((1, 1, BLK, Q_HEADS * BLK), lambda i, j: (i, j, 0, 0)),
            pl.BlockSpec((1, s, KV_HEADS * HEAD_DIM), lambda i, j: (i, 0, 0)),
            pl.BlockSpec((1, KV_HEADS * HEAD_DIM, s), lambda i, j: (i, 0, 0)),
            pl.BlockSpec((1, 1, BLK, IDX_HEADS * BLK), lambda i, j: (i, j, 0, 0)),
            pl.BlockSpec((1, s, BLK), lambda i, j: (i, 0, 0)),
            pl.BlockSpec((1, IDX_HEADS, BLK), lambda i, j: (i, 0, j)),
        ],
        out_specs=pl.BlockSpec((1, BLK, Q_HEADS * HEAD_DIM), lambda i, j: (i, j, 0)),
        out_shape=jax.ShapeDtypeStruct((b, s, Q_HEADS * HEAD_DIM), BF16),
        scratch_shapes=[
            pltpu.VMEM((s, BLK), F32),
            pltpu.VMEM((s, Q_HEADS * BLK), F32),
            pltpu.VMEM((KV_HEADS, HEAD_DIM, GROUP * BLK), F32),
            pltpu.VMEM((8, Q_HEADS * BLK), F32),
            pltpu.VMEM((PAIR, PAIR), BF16),
            pltpu.VMEM((8, BLK), F32),
        ],
        compiler_params=_cparams(("arbitrary", "arbitrary")),
        name="dsa_attention",
    )(qg, k, vt, iq, ik, iw)


def _split3(x):
    hi = x.astype(BF16)
    r1 = x - hi.astype(F32)
    mid = r1.astype(BF16)
    lo = (r1 - mid.astype(F32)).astype(BF16)
    return hi, mid, lo


def _hgrn_chunk(qraw, flog, v, lb, state_t):
    c = B_CHUNK
    q = qraw * jax.nn.sigmoid(qraw)
    f = lb + (1.0 - lb) * jax.nn.sigmoid(flog)
    k = 1.0 - f
    logf = jnp.log(f)
    row = lax.broadcasted_iota(jnp.int32, (c, c), 0)
    lane = lax.broadcasted_iota(jnp.int32, (c, c), 1)
    incl = (lane <= row).astype(BF16)
    b = sum(jnp.dot(incl, part, preferred_element_type=F32) for part in _split3(logf))

    att = jnp.zeros((c, c), F32)
    sub_pos = row & (B_SUB - 1)
    kf = k
    for delta in range(B_SUB):
        if delta > 0:
            kf = pltpu.roll(kf, 1, axis=0) * f
        diag = jnp.sum(q * kf, axis=-1, keepdims=True)
        att = jnp.where((lane == row - delta) & (sub_pos >= delta), diag, att)

    rows = [jnp.zeros((B_SUB, c), F32)]
    for i in range(1, c // B_SUB):
        b_i = b[i * B_SUB - 1:i * B_SUB, :]
        kt = (k * jnp.exp(jnp.minimum(b_i - b, 0.0))).astype(BF16)
        qt = (q[i * B_SUB:(i + 1) * B_SUB] * jnp.exp(b[i * B_SUB:(i + 1) * B_SUB] - b_i)).astype(BF16)
        rows.append(lax.dot_general(qt, kt, (((1,), (1,)), ((), ())), preferred_element_type=F32))
    cross = jnp.concatenate(rows, axis=0)
    att = jnp.where((lane // B_SUB) < (row // B_SUB), cross, att)

    vb = v.astype(BF16)
    o = jnp.dot(att.astype(BF16), vb, preferred_element_type=F32)
    o = o + lax.dot_general((q * jnp.exp(b)).astype(BF16), state_t.astype(BF16), (((1,), (1,)), ((), ())),
                            preferred_element_type=F32)
    b_last = b[c - 1:c, :]
    kd = (k * jnp.exp(b_last - b)).astype(BF16)
    new_state_t = state_t * jnp.exp(b_last) + lax.dot_general(vb, kd, (((0,), (0,)), ((), ())),
                                                              preferred_element_type=F32)
    return o, new_state_t


def _hgrn_kernel(q_ref, f_ref, v_ref, g_ref, lbl_ref, gain_ref, o_ref, state_ref, *, layer, ts):
    @pl.when(pl.program_id(1) == 0)
    def _():
        state_ref[...] = jnp.zeros(state_ref.shape, F32)

    lbl = lbl_ref[...]
    e = jnp.exp(lbl - jnp.max(lbl, axis=0, keepdims=True))
    lb_all = jnp.sum(e[:layer + 1], axis=0, keepdims=True) / jnp.sum(e, axis=0, keepdims=True)
    gain = gain_ref[...]

    def chunk_body(ci, carry):
        r0 = pl.multiple_of(ci * B_CHUNK, B_CHUNK)
        for h in range(B_HEADS):
            cs = slice(h * B_HEAD_DIM, (h + 1) * B_HEAD_DIM)
            o, st = _hgrn_chunk(q_ref[0, pl.ds(r0, B_CHUNK), cs], f_ref[0, pl.ds(r0, B_CHUNK), cs],
                                v_ref[0, pl.ds(r0, B_CHUNK), cs], lb_all[:, cs], state_ref[h])
            state_ref[h] = st
            ms = jnp.mean(o * o, axis=-1, keepdims=True)
            o = (o * lax.rsqrt(ms + RMS_EPS)) * gain
            gate = g_ref[0, pl.ds(r0, B_CHUNK), cs]
            o_ref[0, pl.ds(r0, B_CHUNK), cs] = (o * (gate * jax.nn.sigmoid(gate))).astype(o_ref.dtype)
        return carry

    lax.fori_loop(0, ts // B_CHUNK, chunk_body, 0, unroll=4)


def hgrn2_mixer(zb, lb_logits, out_gain, layer, *, ts=256):
    b, s, _ = zb.shape
    width = B_HEADS * B_HEAD_DIM
    ts = min(ts, s)
    spec = lambda c: pl.BlockSpec((1, ts, width), lambda i, t, c=c: (i, t, c))
    return pl.pallas_call(
        functools.partial(_hgrn_kernel, layer=layer, ts=ts),
        grid=(b, s // ts),
        in_specs=[spec(0), spec(1), spec(2), spec(3),
                  pl.BlockSpec(lb_logits.shape, lambda i, t: (0, 0)),
                  pl.BlockSpec((1, B_HEAD_DIM), lambda i, t: (0, 0))],
        out_specs=pl.BlockSpec((1, ts, width), lambda i, t: (i, t, 0)),
        out_shape=jax.ShapeDtypeStruct((b, s, width), BF16),
        scratch_shapes=[pltpu.VMEM((B_HEADS, B_HEAD_DIM, B_HEAD_DIM), F32)],
        compiler_params=_cparams(("arbitrary", "arbitrary")),
        name="hgrn2_mixer",
    )(zb, zb, zb, zb, lb_logits.astype(F32), out_gain.reshape(1, B_HEAD_DIM).astype(F32))


def _shift_rows(x, prev8, k, row8):
    if k == 0:
        return x
    rolled = pltpu.roll(x, k, axis=0)
    head = jnp.where(row8 < k, pltpu.roll(prev8, k, axis=0), rolled[:8])
    return jnp.concatenate([head, rolled[8:]], axis=0)


def _rglru_kernel(g_ref, x_ref, cw_ref, cb_ref, wr_ref, br_ref, wi_ref, bi_ref, lam_ref, o_ref,
                  xprev_ref, hprev_ref, *, ts):
    @pl.when(pl.program_id(1) == 0)
    def _():
        xprev_ref[...] = jnp.zeros(xprev_ref.shape, F32)
        hprev_ref[...] = jnp.zeros(hprev_ref.shape, F32)

    x = x_ref[0]
    width = x.shape[1]
    prev8 = xprev_ref[...]
    row8 = lax.broadcasted_iota(jnp.int32, (8, width), 0)
    cw = cw_ref[...]
    xc = cb_ref[...] + sum(_shift_rows(x, prev8, k, row8) * cw[CONV_WIDTH - 1 - k:CONV_WIDTH - k, :]
                           for k in range(CONV_WIDTH))
    xprev_ref[...] = x[ts - 8:]

    xb = xc.astype(BF16)
    r = jax.nn.sigmoid(jnp.dot(xb, wr_ref[...], preferred_element_type=F32) + br_ref[...])
    ig = jax.nn.sigmoid(jnp.dot(xb, wi_ref[...], preferred_element_type=F32) + bi_ref[...])
    lam = lam_ref[...]
    softplus_neg = jnp.maximum(-lam, 0.0) + jnp.log(1.0 + jnp.exp(-jnp.abs(lam)))
    a = jnp.exp((-RG_C * r) * softplus_neg)
    u = jnp.sqrt(1.0 - a * a) * (ig * xc)

    pos = lax.broadcasted_iota(jnp.int32, (ts, width), 0) & 7
    d = 1
    while d < 8:
        keep = pos >= d
        u = jnp.where(keep, a * pltpu.roll(u, d, axis=0) + u, u)
        a = jnp.where(keep, a * pltpu.roll(a, d, axis=0), a)
        d *= 2
    carry = hprev_ref[0:1, :]
    hs = []
    for g in range(ts // 8):
        hg = u[8 * g:8 * g + 8] + a[8 * g:8 * g + 8] * carry
        carry = hg[7:8]
        hs.append(hg)
    h = jnp.concatenate(hs, axis=0)
    hprev_ref[...] = jnp.broadcast_to(carry, hprev_ref.shape)

    gt = g_ref[0]
    gelu = 0.5 * gt * (1.0 + jnp.tanh(np.float32(math.sqrt(2.0 / math.pi)) * (gt + 0.044715 * (gt * gt * gt))))
    o_ref[0] = (gelu * h).astype(o_ref.dtype)


def rglru_mixer(zn, conv_w, conv_b, wr_bd, br, wi_bd, bi, lam, *, ts=256):
    b, s, two_c = zn.shape
    c = two_c // 2
    ts = min(ts, s)
    vec = lambda a: a.reshape(1, c).astype(F32)
    const = lambda shape: pl.BlockSpec(shape, lambda i, t: (0, 0))
    return pl.pallas_call(
        functools.partial(_rglru_kernel, ts=ts),
        grid=(b, s // ts),
        in_specs=[pl.BlockSpec((1, ts, c), lambda i, t: (i, t, 0)),
                  pl.BlockSpec((1, ts, c), lambda i, t: (i, t, 1)),
                  const((CONV_WIDTH, c)), const((1, c)), const((c, c)), const((1, c)),
                  const((c, c)), const((1, c)), const((1, c))],
        out_specs=pl.BlockSpec((1, ts, c), lambda i, t: (i, t, 0)),
        out_shape=jax.ShapeDtypeStruct((b, s, c), BF16),
        scratch_shapes=[pltpu.VMEM((8, c), F32), pltpu.VMEM((8, c), F32)],
        compiler_params=_cparams(("arbitrary", "arbitrary")),
        name="rglru_mixer",
    )(zn, zn, conv_w.astype(F32), vec(conv_b), wr_bd, vec(br), wi_bd, vec(bi), vec(lam))


def _block_diag(w):
    n, c, d = w.shape
    eye = jnp.eye(n, dtype=w.dtype)
    return (eye[:, None, :, None] * w[:, :, None, :]).reshape(n * c, n * d)


SWA_BLOCKS_PER_STEP = 4


def _swa_kernel(q_ref, kp_ref, kc_ref, vp_ref, vc_ref, sink_ref, o_ref, *, nblk):
    j = pl.program_id(1)
    row = lax.broadcasted_iota(jnp.int32, (PAIR, BLK), 0)
    col = lax.broadcasted_iota(jnp.int32, (PAIR, BLK), 1)
    bias = jnp.where(row < BLK, jnp.where(row > col, 0.0, NEG_BIG), jnp.where(row - BLK <= col, 0.0, NEG_BIG))
    bias_first = jnp.where(j > 0, bias, jnp.where(row < BLK, NEG_BIG, bias))
    bias = jnp.concatenate([bias] * GROUP, axis=1)
    bias_first = jnp.concatenate([bias_first] * GROUP, axis=1)
    for u in range(nblk):
        outs = []
        for g in range(KV_HEADS):
            rows = slice(g * HEAD_DIM, (g + 1) * HEAD_DIM)
            if u == 0:
                keys = jnp.concatenate([kp_ref[0], kc_ref[0, 0:BLK, :]], axis=0)
                vals = jnp.concatenate([vp_ref[0, rows, :], vc_ref[0, rows, 0:BLK]], axis=1)
            else:
                keys = kc_ref[0, (u - 1) * BLK:(u + 1) * BLK, :]
                vals = vc_ref[0, rows, (u - 1) * BLK:(u + 1) * BLK]
            q = q_ref[0, u, :, g * GROUP * BLK:(g + 1) * GROUP * BLK]
            lg = jnp.dot(keys, q, preferred_element_type=F32) + (bias_first if u == 0 else bias)
            sink = sink_ref[g] * LOG2E
            m = jnp.maximum(jnp.max(lg, axis=0, keepdims=True), sink)
            p = jnp.exp2(lg - m)
            den = jnp.sum(p, axis=0, keepdims=True) + jnp.exp2(sink - m)
            outs.append(jnp.dot(vals, p.astype(BF16), preferred_element_type=F32) / den)
        for pair in range(Q_HEADS // 2):
            g, hh = divmod(2 * pair, GROUP)
            o = outs[g]
            two = jnp.concatenate([o[:, hh * BLK:(hh + 1) * BLK], o[:, (hh + 1) * BLK:(hh + 2) * BLK]], axis=0)
            o_ref[0, u * BLK:(u + 1) * BLK, pair * BLK:(pair + 1) * BLK] = two.T.astype(o_ref.dtype)


def swa_attention(q, k, vt, sinks):
    b, s, _ = k.shape
    nblk = SWA_BLOCKS_PER_STEP
    assert s % (nblk * BLK) == 0
    rows = nblk * BLK
    sink_rows = jnp.repeat(sinks.astype(F32), BLK).reshape(KV_HEADS, 1, GROUP * BLK)
    prev = lambda j: jnp.maximum(nblk * j - 1, 0)
    return pl.pallas_call(
        functools.partial(_swa_kernel, nblk=nblk),
        grid=(b, s // rows),
        in_specs=[
            pl.BlockSpec((1, nblk, BLK, Q_HEADS * BLK), lambda i, j: (i, j, 0, 0)),
            pl.BlockSpec((1, BLK, KV_HEADS * HEAD_DIM), lambda i, j: (i, prev(j), 0)),
            pl.BlockSpec((1, rows, KV_HEADS * HEAD_DIM), lambda i, j: (i, j, 0)),
            pl.BlockSpec((1, KV_HEADS * HEAD_DIM, BLK), lambda i, j: (i, 0, prev(j))),
            pl.BlockSpec((1, KV_HEADS * HEAD_DIM, rows), lambda i, j: (i, 0, j)),
            pl.BlockSpec((KV_HEADS, 1, GROUP * BLK), lambda i, j: (0, 0, 0)),
        ],
        out_specs=pl.BlockSpec((1, rows, Q_HEADS * HEAD_DIM), lambda i, j: (i, j, 0)),
        out_shape=jax.ShapeDtypeStruct((b, s, Q_HEADS * HEAD_DIM), BF16),
        compiler_params=_cparams(("arbitrary", "arbitrary")),
        name="swa_attention",
    )(q, k, k, vt, vt, sink_rows)


def _mlp_kernel(x_ref, oa_ref, ob_ref, wo_ref, g_ref, wu_ref, wd_ref, out_ref, *, hc):
    half = oa_ref.shape[1]
    x1 = (x_ref[...] + jnp.dot(oa_ref[...], wo_ref[:half, :], preferred_element_type=F32)
          + jnp.dot(ob_ref[...], wo_ref[half:, :], preferred_element_type=F32))
    ms = jnp.mean(x1 * x1, axis=-1, keepdims=True)
    hn = ((x1 * lax.rsqrt(ms + RMS_EPS)) * g_ref[...]).astype(BF16)
    out_ref[...] = x1
    hidden = wu_ref.shape[1]
    for c in range(hidden // hc):
        up = jnp.dot(hn, wu_ref[:, c * hc:(c + 1) * hc], preferred_element_type=F32)
        act = jnp.square(jnp.maximum(up, 0.0)).astype(BF16)
        out_ref[...] += jnp.dot(act, wd_ref[c * hc:(c + 1) * hc, :], preferred_element_type=F32)


def out_proj_mlp(x, oa, ob, w_out, gain, w_up, w_down, *, tm=512, hc=1024):
    b, s, d = x.shape
    t = b * s
    tm = min(tm, t)
    half = oa.shape[-1]
    hidden = w_up.shape[1]
    const = lambda shape: pl.BlockSpec(shape, lambda i: (0, 0), pipeline_mode=pl.Buffered(1))
    out = pl.pallas_call(
        functools.partial(_mlp_kernel, hc=hc),
        grid=(t // tm,),
        in_specs=[pl.BlockSpec((tm, d), lambda i: (i, 0)),
                  pl.BlockSpec((tm, half), lambda i: (i, 0)),
                  pl.BlockSpec((tm, half), lambda i: (i, 0)),
                  const((d, d)), const((1, d)), const((d, hidden)), const((hidden, d))],
        out_specs=pl.BlockSpec((tm, d), lambda i: (i, 0)),
        out_shape=jax.ShapeDtypeStruct((t, d), F32),
        compiler_params=_cparams(("arbitrary",)),
        name="out_proj_mlp",
    )(x.reshape(t, d), oa.reshape(t, half), ob.reshape(t, half), w_out.astype(BF16),
      gain.reshape(1, d).astype(F32), w_up.astype(BF16), w_down.astype(BF16))
    return out.reshape(b, s, d)


N_EVEN_T = Q_HEADS * HEAD_DIM + 2 * KV_HEADS * HEAD_DIM + IDX_HEADS * HEAD_DIM + HEAD_DIM + IDX_HEADS
N_ODD_NAT = 2 * Q_HEADS * HEAD_DIM


def layer0_attention(x, pos, inp, j, rope=None):
    cos_t, sin_t = rope if rope is not None else rope_tables_t(pos)
    w_in = inp['even_w_in'][j]
    w_t = w_in[:, :N_EVEN_T].T.astype(BF16)
    w_n = w_in[:, N_EVEN_T:].astype(BF16)
    qg, k, vt, iq, ik, iw, zb = input_projection(
        x, inp['norm_mix_g'][2 * j], w_t, w_n, cos_t, sin_t, inp['a_q_norm_g'][j], inp['a_k_norm_g'][j],
        with_indexer=True)
    return dsa_attention(qg, k, vt, iq, ik, iw), zb


def layer1_mixers(x, pos, inp, j, rope=None):
    cos_t, sin_t = rope if rope is not None else rope_tables_t(pos)
    w_in = inp['odd_w_in'][j]
    w_n = w_in[:, :N_ODD_NAT].astype(BF16)
    w_t = w_in[:, N_ODD_NAT:].T.astype(BF16)
    qg, k, vt, zn = input_projection(
        x, inp['norm_mix_g'][2 * j + 1], w_t, w_n, cos_t, sin_t, inp['d_q_norm_g'][j], inp['d_k_norm_g'][j],
        with_indexer=False)
    o_c = rglru_mixer(zn, inp['c_conv_w'][j], inp['c_conv_b'][j],
                      _block_diag(inp['c_rgate_w'][j]).astype(BF16), inp['c_rgate_b'][j],
                      _block_diag(inp['c_igate_w'][j]).astype(BF16), inp['c_igate_b'][j], inp['c_lambda'][j])
    o_d = swa_attention(qg, k, vt, inp['d_sinks'][j])
    return o_c, o_d


def kernel(x, positions, norm_mix_g, norm_mlp_g, even_w_in, even_w_out, a_q_norm_g, a_k_norm_g, b_lb_logits,
           b_out_norm_g, odd_w_in, odd_w_out, c_conv_w, c_conv_b, c_rgate_w, c_rgate_b, c_igate_w, c_igate_b,
           c_lambda, d_q_norm_g, d_k_norm_g, d_sinks, mlp_w_up, mlp_w_down):
    inp = dict(norm_mix_g=norm_mix_g, even_w_in=even_w_in, a_q_norm_g=a_q_norm_g, a_k_norm_g=a_k_norm_g,
               odd_w_in=odd_w_in, c_conv_w=c_conv_w, c_conv_b=c_conv_b, c_rgate_w=c_rgate_w, c_rgate_b=c_rgate_b,
               c_igate_w=c_igate_w, c_igate_b=c_igate_b, c_lambda=c_lambda, d_q_norm_g=d_q_norm_g,
               d_k_norm_g=d_k_norm_g, d_sinks=d_sinks)
    depth = norm_mix_g.shape[0]
    rope = rope_tables_t(positions)
    for layer in range(depth):
        j = layer // 2
        if layer % 2 == 0:
            o_a, zb = layer0_attention(x, positions, inp, j, rope)
            o_b = hgrn2_mixer(zb, b_lb_logits, b_out_norm_g[j], j)
            x = out_proj_mlp(x, o_a, o_b, even_w_out[j], norm_mlp_g[layer], mlp_w_up[layer], mlp_w_down[layer])
        else:
            o_c, o_d = layer1_mixers(x, positions, inp, j, rope)
            x = out_proj_mlp(x, o_c, o_d, odd_w_out[j], norm_mlp_g[layer], mlp_w_up[layer], mlp_w_down[layer])
    return x
```

```python
import functools
import math

import numpy as np
import jax
import jax.numpy as jnp
from jax import lax
from jax.experimental import pallas as pl
from jax.experimental.pallas import tpu as pltpu

F32 = jnp.float32
BF16 = jnp.bfloat16

HEAD_DIM = 64
ROPE_THETA = 10000.0
RMS_EPS = 1e-6
KV_HEADS = 2
Q_HEADS = 8
GROUP = Q_HEADS // KV_HEADS
IDX_HEADS = 8
DSA_TOPK = 256
BLK = 128
B_HEAD_DIM = 128
B_HEADS = 4
B_CHUNK = 64
B_SUB = 16
CONV_WIDTH = 4
RG_C = 8.0
NEG_BIG = -1e30
LOG2E = math.log2(math.e)

VMEM_LIMIT = 56 * 1024 * 1024


def _cparams(sem):
    return pltpu.CompilerParams(dimension_semantics=sem, vmem_limit_bytes=VMEM_LIMIT)


def _rope_kernel(pos_ref, invf_ref, cos_ref, sin_ref):
    ang = pos_ref[0].astype(F32) * invf_ref[...]
    cos_ref[0] = jnp.cos(ang)
    sin_ref[0] = jnp.sin(ang)


def rope_tables_t(positions):
    b, s = positions.shape
    half = HEAD_DIM // 2
    invf = (1.0 / (ROPE_THETA ** (jnp.arange(0, HEAD_DIM, 2, dtype=F32) / HEAD_DIM))).reshape(half, 1)
    out = jax.ShapeDtypeStruct((b, half, s), F32)
    return pl.pallas_call(
        _rope_kernel,
        grid=(b,),
        in_specs=[pl.BlockSpec((1, 1, s), lambda i: (i, 0, 0)),
                  pl.BlockSpec((half, 1), lambda i: (0, 0))],
        out_specs=[pl.BlockSpec((1, half, s), lambda i: (i, 0, 0))] * 2,
        out_shape=[out, out],
        compiler_params=_cparams(("arbitrary",)),
        name="rope_tables",
    )(positions.reshape(b, 1, s), invf)


def _head_norm_rope_t(blk, gain_col, cos, sin, scale):
    if gain_col is not None:
        ms = jnp.mean(blk * blk, axis=0, keepdims=True)
        blk = (blk * lax.rsqrt(ms + RMS_EPS)) * gain_col
    half = HEAD_DIM // 2
    x1, x2 = blk[:half], blk[half:]
    o1 = x1 * cos - x2 * sin
    o2 = x2 * cos + x1 * sin
    out = jnp.concatenate([o1, o2], axis=0)
    if scale != 1.0:
        out = out * scale
    return out


def _proj_kernel(*refs, tm, with_indexer):
    if with_indexer:
        (x_ref, g_ref, wt_ref, wn_ref, cos_ref, sin_ref, qgain_ref, kgain_ref,
         qg_ref, k_ref, vt_ref, iq_ref, ik_ref, iw_ref, nat_ref) = refs
    else:
        (x_ref, g_ref, wt_ref, wn_ref, cos_ref, sin_ref, qgain_ref, kgain_ref,
         qg_ref, k_ref, vt_ref, nat_ref) = refs
    nq = tm // BLK
    zts = []
    half = tm // 2
    for part in range(2):
        x = x_ref[0, part * half:(part + 1) * half, :]
        ms = jnp.mean(x * x, axis=-1, keepdims=True)
        hn = ((x * lax.rsqrt(ms + RMS_EPS)) * g_ref[...]).astype(BF16)
        zts.append(lax.dot_general(wt_ref[...], hn, (((1,), (1,)), ((), ())), preferred_element_type=F32))
        nat_ref[0, part * half:(part + 1) * half, :] = jnp.dot(hn, wn_ref[...], preferred_element_type=F32)
    zt = jnp.concatenate(zts, axis=1)
    cos = cos_ref[0]
    sin = sin_ref[0]
    qgain = qgain_ref[...]
    kgain = kgain_ref[...]
    zeros64 = jnp.zeros((HEAD_DIM, BLK), BF16)
    r = 0
    for h in range(Q_HEADS):
        g, hh = divmod(h, GROUP)
        q = _head_norm_rope_t(zt[r:r + HEAD_DIM], qgain, cos, sin, HEAD_DIM ** -0.5 * LOG2E).astype(BF16)
        r += HEAD_DIM
        for j in range(nq):
            qg_ref[0, j, g * HEAD_DIM:(g + 1) * HEAD_DIM, h * BLK:(h + 1) * BLK] = q[:, j * BLK:(j + 1) * BLK]
            qg_ref[0, j, (1 - g) * HEAD_DIM:(2 - g) * HEAD_DIM, h * BLK:(h + 1) * BLK] = zeros64
    ks = []
    for g in range(KV_HEADS):
        ks.append(_head_norm_rope_t(zt[r:r + HEAD_DIM], kgain, cos, sin, 1.0))
        r += HEAD_DIM
    k_ref[0] = jnp.concatenate(ks, axis=0).T.astype(BF16)
    vt_ref[0] = zt[r:r + KV_HEADS * HEAD_DIM].astype(BF16)
    r += KV_HEADS * HEAD_DIM
    if with_indexer:
        for h in range(IDX_HEADS):
            iq = _head_norm_rope_t(zt[r:r + HEAD_DIM], None, cos, sin, 1.0).astype(BF16)
            r += HEAD_DIM
            for j in range(nq):
                iq_ref[0, j, 0:HEAD_DIM, h * BLK:(h + 1) * BLK] = iq[:, j * BLK:(j + 1) * BLK]
                iq_ref[0, j, HEAD_DIM:2 * HEAD_DIM, h * BLK:(h + 1) * BLK] = zeros64
        ikt = _head_norm_rope_t(zt[r:r + HEAD_DIM], None, cos, sin, 1.0)
        r += HEAD_DIM
        ik_ref[0] = jnp.concatenate([ikt, jnp.zeros_like(ikt)], axis=0).T.astype(BF16)
        iw_ref[0] = zt[r:r + IDX_HEADS] * (HEAD_DIM ** -0.5 * IDX_HEADS ** -0.5)
        r += IDX_HEADS


def input_projection(x, gain, w_t, w_n, cos_t, sin_t, q_gain, k_gain, *, with_indexer, tm=512):
    b, s, d = x.shape
    tm = min(tm, s)
    nt = w_t.shape[0]
    n_nat = w_n.shape[1]
    nqb = s // BLK
    half = HEAD_DIM // 2
    grid = (b, s // tm)
    in_specs = [
        pl.BlockSpec((1, tm, d), lambda i, j: (i, j, 0)),
        pl.BlockSpec((1, d), lambda i, j: (0, 0)),
        pl.BlockSpec((nt, d), lambda i, j: (0, 0)),
        pl.BlockSpec((d, n_nat), lambda i, j: (0, 0)),
        pl.BlockSpec((1, half, tm), lambda i, j: (i, 0, j)),
        pl.BlockSpec((1, half, tm), lambda i, j: (i, 0, j)),
        pl.BlockSpec((HEAD_DIM, 1), lambda i, j: (0, 0)),
        pl.BlockSpec((HEAD_DIM, 1), lambda i, j: (0, 0)),
    ]
    out_shape = [
        jax.ShapeDtypeStruct((b, nqb, BLK, Q_HEADS * BLK), BF16),
        jax.ShapeDtypeStruct((b, s, KV_HEADS * HEAD_DIM), BF16),
        jax.ShapeDtypeStruct((b, KV_HEADS * HEAD_DIM, s), BF16),
    ]
    out_specs = [
        pl.BlockSpec((1, tm // BLK, BLK, Q_HEADS * BLK), lambda i, j: (i, j, 0, 0)),
        pl.BlockSpec((1, tm, KV_HEADS * HEAD_DIM), lambda i, j: (i, j, 0)),
        pl.BlockSpec((1, KV_HEADS * HEAD_DIM, tm), lambda i, j: (i, 0, j)),
    ]
    if with_indexer:
        out_shape += [
            jax.ShapeDtypeStruct((b, nqb, BLK, IDX_HEADS * BLK), BF16),
            jax.ShapeDtypeStruct((b, s, BLK), BF16),
            jax.ShapeDtypeStruct((b, IDX_HEADS, s), F32),
        ]
        out_specs += [
            pl.BlockSpec((1, tm // BLK, BLK, IDX_HEADS * BLK), lambda i, j: (i, j, 0, 0)),
            pl.BlockSpec((1, tm, BLK), lambda i, j: (i, j, 0)),
            pl.BlockSpec((1, IDX_HEADS, tm), lambda i, j: (i, 0, j)),
        ]
    out_shape.append(jax.ShapeDtypeStruct((b, s, n_nat), F32))
    out_specs.append(pl.BlockSpec((1, tm, n_nat), lambda i, j: (i, j, 0)))
    return pl.pallas_call(
        functools.partial(_proj_kernel, tm=tm, with_indexer=with_indexer),
        grid=grid,
        in_specs=in_specs,
        out_specs=out_specs,
        out_shape=out_shape,
        compiler_params=_cparams(("arbitrary", "arbitrary")),
        name="in_proj_dsa" if with_indexer else "in_proj_swa",
    )(x, gain.reshape(1, d), w_t, w_n, cos_t, sin_t, q_gain.reshape(HEAD_DIM, 1), k_gain.reshape(HEAD_DIM, 1))


PAIR = 2 * BLK
QUAD = 4 * BLK
COUNT_ROWS = 64
PAIRS_PER_TRIP = (8, 4, 2, 1)
FLT_LOWEST = float(np.finfo(np.float32).min)
SEARCH_STEPS = 16
SEARCH_STEPS_PER_TRIP = 4


def _fold8(x):
    return x.reshape(x.shape[0] // 8, 8, x.shape[1])


def _pair_loop(npair, pair_fn, carry):
    start = 0
    for width in PAIRS_PER_TRIP:
        trips = (npair - start) // width

        def body(i, c, start=start, width=width):
            for u in range(width):
                c = pair_fn(start + width * i + u, c)
            return c
        carry = lax.fori_loop(0, trips, body, carry)
        start = start + width * trips
    return carry


def _count(sc_ref, nquad, thr):
    def body(i, acc):
        off = pl.multiple_of(i * QUAD, QUAD)
        hit = jnp.where(sc_ref[pl.ds(off, QUAD), :] >= thr, 1.0, 0.0)
        return acc + jnp.sum(hit.reshape(QUAD // COUNT_ROWS, COUNT_ROWS, BLK), axis=0)
    acc = lax.fori_loop(0, nquad, body, jnp.zeros((COUNT_ROWS, BLK), F32))
    return jnp.sum(acc, axis=0, keepdims=True)


def _resolve_surplus(sc_ref, nquad, thr, excess):
    def smallest_selected():
        def min_body(i, acc):
            off = pl.multiple_of(i * QUAD, QUAD)
            blk = sc_ref[pl.ds(off, QUAD), :]
            return jnp.minimum(acc, jnp.min(_fold8(jnp.where(blk >= thr, blk, jnp.inf)), axis=0))
        low = jnp.min(lax.fori_loop(0, nquad, min_body, jnp.full((8, BLK), jnp.inf, F32)), axis=0, keepdims=True)

        def cnt_body(i, acc):
            off = pl.multiple_of(i * QUAD, QUAD)
            hit = jnp.where(sc_ref[pl.ds(off, QUAD), :] == low, 1.0, 0.0)
            return acc + jnp.sum(hit.reshape(QUAD // COUNT_ROWS, COUNT_ROWS, BLK), axis=0)
        copies = jnp.sum(lax.fori_loop(0, nquad, cnt_body, jnp.zeros((COUNT_ROWS, BLK), F32)), axis=0, keepdims=True)
        return low, copies

    def cond(st):
        ex, _, copies = st
        return jnp.max(jnp.where(ex > copies, 1.0, 0.0)) > 0.0

    def body(st):
        ex, low, copies = st
        whole = ex > copies
        gone = jnp.where(whole, low, jnp.inf)

        def drop_body(i, carry):
            off = pl.multiple_of(i * QUAD, QUAD)
            blk = sc_ref[pl.ds(off, QUAD), :]
            sc_ref[pl.ds(off, QUAD), :] = jnp.where(blk == gone, -jnp.inf, blk)
            return carry
        lax.fori_loop(0, nquad, drop_body, 0)
        ex = jnp.where(whole, ex - copies, ex)
        low, copies = smallest_selected()
        return ex, low, copies

    low, copies = smallest_selected()
    ex, low, copies = lax.while_loop(cond, body, (excess, low, copies))
    surplus = ex > 0.0
    return jnp.where(surplus, low, jnp.inf), jnp.where(surplus, copies - ex, 0.0)


def _search_step(sc_ref, nquad, topk, st):
    lo, hi, c_lo, c_hi, done = st
    guess = lo + (hi - lo) * 0.5
    inside = (guess > lo) & (guess < hi)
    c = _count(sc_ref, nquad, guess)
    move = inside & (done == 0.0)
    up = move & (c >= topk)
    down = move & (c < topk)
    lo = jnp.where(up, guess, lo)
    c_lo = jnp.where(up, c, c_lo)
    hi = jnp.where(down, guess, hi)
    c_hi = jnp.where(down, c, c_hi)
    done = jnp.where(inside & (c_lo != topk), done, 1.0)
    return lo, hi, c_lo, c_hi, done


def _dsa_kernel(q_ref, k_ref, vt_ref, iq_ref, ik_ref, iw_ref, o_ref, sc_ref, lg_ref, acc_ref, mx_ref, tri_ref,
                tie_ref, *, topk):
    qi = pl.program_id(1)
    npair = (qi + 2) // 2
    iq = iq_ref[0, 0]
    w = iw_ref[0]
    nquad = (qi + 4) // 4
    row2 = lax.broadcasted_iota(jnp.int32, (PAIR, BLK), 0)
    col2 = lax.broadcasted_iota(jnp.int32, (PAIR, BLK), 1)

    def score_body(pi, carry):
        lo8, hi8 = carry
        off = pl.multiple_of(pi * PAIR, PAIR)
        ikb = ik_ref[0, pl.ds(off, PAIR), :]
        tot = jnp.zeros((PAIR, BLK), F32)
        for h in range(IDX_HEADS):
            sc = jnp.dot(ikb, iq_ref[0, 0, :, h * BLK:(h + 1) * BLK], preferred_element_type=F32)
            tot = tot + jnp.maximum(sc, 0.0) * w[h:h + 1, :]
        causal = (pi * PAIR + row2) <= (qi * BLK + col2)
        sc_ref[pl.ds(off, PAIR), :] = jnp.where(causal, tot, -jnp.inf)
        lo8 = jnp.minimum(lo8, jnp.min(_fold8(jnp.where(causal, tot, jnp.inf)), axis=0))
        hi8 = jnp.maximum(hi8, jnp.max(_fold8(jnp.where(causal, tot, -jnp.inf)), axis=0))
        return lo8, hi8

    lo8, hi8 = _pair_loop(npair, score_body,
                          (jnp.full((8, BLK), jnp.inf, F32), jnp.full((8, BLK), -jnp.inf, F32)))

    @pl.when(2 * npair < 4 * nquad)
    def _():
        sc_ref[pl.ds(pl.multiple_of(npair * PAIR, PAIR), PAIR), :] = jnp.full((PAIR, BLK), -jnp.inf, F32)

    smin = jnp.min(lo8, axis=0, keepdims=True)
    smax = jnp.max(hi8, axis=0, keepdims=True)
    n_valid = (qi * BLK + 1 + lax.broadcasted_iota(jnp.int32, (1, BLK), 1)).astype(F32)
    few = n_valid <= topk
    above = smax + (jnp.abs(smax) * 1e-6 + 1e-30)
    state = (smin, above, n_valid, jnp.zeros((1, BLK), F32), jnp.where(few, 1.0, 0.0))

    def search_body(i, st):
        for _ in range(SEARCH_STEPS_PER_TRIP):
            st = _search_step(sc_ref, nquad, topk, st)
        return st

    lo, _, c_lo, _, _ = lax.fori_loop(0, SEARCH_STEPS // SEARCH_STEPS_PER_TRIP, search_body, state)
    thr = jnp.where(few, FLT_LOWEST, lo)
    excess = jnp.where(few, 0.0, c_lo - topk)

    def logits_of_pair(off, bias, mx):
        kb = k_ref[0, pl.ds(off, PAIR), :]
        parts = []
        for h in range(Q_HEADS):
            lgh = jnp.dot(kb, q_ref[0, 0, :, h * BLK:(h + 1) * BLK], preferred_element_type=F32) + bias
            lg_ref[pl.ds(off, PAIR), h * BLK:(h + 1) * BLK] = lgh
            parts.append(jnp.max(_fold8(lgh), axis=0))
        return jnp.maximum(mx, jnp.concatenate(parts, axis=1))

    mx0 = jnp.full((8, Q_HEADS * BLK), NEG_BIG, F32)

    tie_ref[0:1, :] = jnp.full((1, BLK), jnp.inf, F32)
    tie_ref[1:2, :] = jnp.zeros((1, BLK), F32)

    @pl.when(jnp.max(excess) > 0.0)
    def _():
        low, keep = _resolve_surplus(sc_ref, nquad, thr, excess)
        tie_ref[0:1, :] = low
        tie_ref[1:2, :] = keep

    low = tie_ref[0:1, :]
    keep = tie_ref[1:2, :]
    need_rank = jnp.max(keep) > 0.0

    @pl.when(jnp.logical_not(need_rank))
    def _():
        def logit_body(pi, mx):
            off = pl.multiple_of(pi * PAIR, PAIR)
            sc = sc_ref[pl.ds(off, PAIR), :]
            return logits_of_pair(off, jnp.where((sc >= thr) & (sc != low), 0.0, NEG_BIG), mx)
        mx_ref[...] = _pair_loop(npair, logit_body, mx0)

    @pl.when(need_rank)
    def _():
        row = lax.broadcasted_iota(jnp.int32, (PAIR, PAIR), 0)
        col = lax.broadcasted_iota(jnp.int32, (PAIR, PAIR), 1)
        tri_ref[...] = (col < row).astype(BF16)

        def logit_body(pi, carry):
            mx, seen = carry
            off = pl.multiple_of(pi * PAIR, PAIR)
            sc = sc_ref[pl.ds(off, PAIR), :]
            eq = sc == low
            eqf = jnp.where(eq, 1.0, 0.0)
            rank = jnp.dot(tri_ref[...], eqf.astype(BF16), preferred_element_type=F32) + seen
            bias = jnp.where(sc >= thr, 0.0, NEG_BIG)
            bias = jnp.where(eq & (rank >= keep), NEG_BIG, bias)
            return logits_of_pair(off, bias, mx), seen + jnp.sum(eqf, axis=0, keepdims=True)

        mx, _ = _pair_loop(npair, logit_body, (mx0, jnp.zeros((1, BLK), F32)))
        mx_ref[...] = mx

    m = jnp.max(mx_ref[...], axis=0, keepdims=True)

    acc_ref[...] = jnp.zeros(acc_ref.shape, F32)

    def pv_body(pi, ls):
        off = pl.multiple_of(pi * PAIR, PAIR)
        sums, pbs = [], []
        for h in range(Q_HEADS):
            cs = slice(h * BLK, (h + 1) * BLK)
            p = jnp.exp2(lg_ref[pl.ds(off, PAIR), cs] - m[:, cs])
            sums.append(jnp.sum(_fold8(p), axis=0))
            pbs.append(p.astype(BF16))
        for g in range(KV_HEADS):
            pb = jnp.concatenate(pbs[g * GROUP:(g + 1) * GROUP], axis=1)
            vt = vt_ref[0, g * HEAD_DIM:(g + 1) * HEAD_DIM, pl.ds(off, PAIR)]
            acc_ref[g] += jnp.dot(vt, pb, preferred_element_type=F32)
        return ls + jnp.concatenate(sums, axis=1)

    ls = _pair_loop(npair, pv_body, jnp.zeros((8, Q_HEADS * BLK), F32))
    l = jnp.sum(ls, axis=0, keepdims=True)

    for pair in range(Q_HEADS // 2):
        g, hh = divmod(2 * pair, GROUP)
        o = acc_ref[g] / l[:, g * GROUP * BLK:(g + 1) * GROUP * BLK]
        two = jnp.concatenate([o[:, hh * BLK:(hh + 1) * BLK], o[:, (hh + 1) * BLK:(hh + 2) * BLK]], axis=0)
        o_ref[0, :, pair * BLK:(pair + 1) * BLK] = two.T.astype(o_ref.dtype)


def dsa_attention(qg, k, vt, iq, ik, iw):
    b, s, _ = k.shape
    assert s % QUAD == 0, "the counting loops pad the score tile to whole groups of four key blocks"
    nqb = s // BLK
    topk = min(DSA_TOPK, s // 4)
    return pl.pallas_call(
        functools.partial(_dsa_kernel, topk=topk),
        grid=(b, nqb),
        in_specs=[
            pl.BlockSpec((1, 1, BLK, Q_HEADS * BLK), lambda i, j: (i, j, 0, 0)),
            pl.BlockSpec((1, s, KV_HEADS * HEAD_DIM), lambda i, j: (i, 0, 0)),
            pl.BlockSpec((1, KV_HEADS * HEAD_DIM, s), lambda i, j: (i, 0, 0)),
            pl.BlockSpec((1, 1, BLK, IDX_HEADS * BLK), lambda i, j: (i, j, 0, 0)),
            pl.BlockSpec((1, s, BLK), lambda i, j: (i, 0, 0)),
            pl.BlockSpec((1, IDX_HEADS, BLK), lambda i, j: (i, 0, j)),
        ],
        out_specs=pl.BlockSpec((1, BLK, Q_HEADS * HEAD_DIM), lambda i, j: (i, j, 0)),
        out_shape=jax.ShapeDtypeStruct((b, s, Q_HEADS * HEAD_DIM), BF16),
        scratch_shapes=[
            pltpu.VMEM((s, BLK), F32),
            pltpu.VMEM((s, Q_HEADS * BLK), F32),
            pltpu.VMEM((KV_HEADS, HEAD_DIM, GROUP * BLK), F32),
            pltpu.VMEM((8, Q_HEADS * BLK), F32),
            pltpu.VMEM((PAIR, PAIR), BF16),
            pltpu.VMEM((8, BLK), F32),
        ],
        compiler_params=_cparams(("arbitrary", "arbitrary")),
        name="dsa_attention",
    )(qg, k, vt, iq, ik, iw)


def _split3(x):
    hi = x.astype(BF16)
    r1 = x - hi.astype(F32)
    mid = r1.astype(BF16)
    lo = (r1 - mid.astype(F32)).astype(BF16)
    return hi, mid, lo


def _hgrn_chunk(qraw, flog, v, lb, state_t):
    c = B_CHUNK
    q = qraw * jax.nn.sigmoid(qraw)
    f = lb + (1.0 - lb) * jax.nn.sigmoid(flog)
    k = 1.0 - f
    logf = jnp.log(f)
    row = lax.broadcasted_iota(jnp.int32, (c, c), 0)
    lane = lax.broadcasted_iota(jnp.int32, (c, c), 1)
    incl = (lane <= row).astype(BF16)
    b = sum(jnp.dot(incl, part, preferred_element_type=F32) for part in _split3(logf))

    att = jnp.zeros((c, c), F32)
    sub_pos = row & (B_SUB - 1)
    kf = k
    for delta in range(B_SUB):
        if delta > 0:
            kf = pltpu.roll(kf, 1, axis=0) * f
        diag = jnp.sum(q * kf, axis=-1, keepdims=True)
        att = jnp.where((lane == row - delta) & (sub_pos >= delta), diag, att)

    rows = [jnp.zeros((B_SUB, c), F32)]
    for i in range(1, c // B_SUB):
        b_i = b[i * B_SUB - 1:i * B_SUB, :]
        kt = (k * jnp.exp(jnp.minimum(b_i - b, 0.0))).astype(BF16)
        qt = (q[i * B_SUB:(i + 1) * B_SUB] * jnp.exp(b[i * B_SUB:(i + 1) * B_SUB] - b_i)).astype(BF16)
        rows.append(lax.dot_general(qt, kt, (((1,), (1,)), ((), ())), preferred_element_type=F32))
    cross = jnp.concatenate(rows, axis=0)
    att = jnp.where((lane // B_SUB) < (row // B_SUB), cross, att)

    vb = v.astype(BF16)
    o = jnp.dot(att.astype(BF16), vb, preferred_element_type=F32)
    o = o + lax.dot_general((q * jnp.exp(b)).astype(BF16), state_t.astype(BF16), (((1,), (1,)), ((), ())),
                            preferred_element_type=F32)
    b_last = b[c - 1:c, :]
    kd = (k * jnp.exp(b_last - b)).astype(BF16)
    new_state_t = state_t * jnp.exp(b_last) + lax.dot_general(vb, kd, (((0,), (0,)), ((), ())),
                                                              preferred_element_type=F32)
    return o, new_state_t


def _hgrn_kernel(q_ref, f_ref, v_ref, g_ref, lbl_ref, gain_ref, o_ref, state_ref, *, layer, ts):
    @pl.when(pl.program_id(1) == 0)
    def _():
        state_ref[...] = jnp.zeros(state_ref.shape, F32)

    lbl = lbl_ref[...]
    e = jnp.exp(lbl - jnp.max(lbl, axis=0, keepdims=True))
    lb_all = jnp.sum(e[:layer + 1], axis=0, keepdims=True) / jnp.sum(e, axis=0, keepdims=True)
    gain = gain_ref[...]

    def chunk_body(ci, carry):
        r0 = pl.multiple_of(ci * B_CHUNK, B_CHUNK)
        for h in range(B_HEADS):
            cs = slice(h * B_HEAD_DIM, (h + 1) * B_HEAD_DIM)
            o, st = _hgrn_chunk(q_ref[0, pl.ds(r0, B_CHUNK), cs], f_ref[0, pl.ds(r0, B_CHUNK), cs],
                                v_ref[0, pl.ds(r0, B_CHUNK), cs], lb_all[:, cs], state_ref[h])
            state_ref[h] = st
            ms = jnp.mean(o * o, axis=-1, keepdims=True)
            o = (o * lax.rsqrt(ms + RMS_EPS)) * gain
            gate = g_ref[0, pl.ds(r0, B_CHUNK), cs]
            o_ref[0, pl.ds(r0, B_CHUNK), cs] = (o * (gate * jax.nn.sigmoid(gate))).astype(o_ref.dtype)
        return carry

    lax.fori_loop(0, ts // B_CHUNK, chunk_body, 0, unroll=4)


def hgrn2_mixer(zb, lb_logits, out_gain, layer, *, ts=256):
    b, s, _ = zb.shape
    width = B_HEADS * B_HEAD_DIM
    ts = min(ts, s)
    spec = lambda c: pl.BlockSpec((1, ts, width), lambda i, t, c=c: (i, t, c))
    return pl.pallas_call(
        functools.partial(_hgrn_kernel, layer=layer, ts=ts),
        grid=(b, s // ts),
        in_specs=[spec(0), spec(1), spec(2), spec(3),
                  pl.BlockSpec(lb_logits.shape, lambda i, t: (0, 0)),
                  pl.BlockSpec((1, B_HEAD_DIM), lambda i, t: (0, 0))],
        out_specs=pl.BlockSpec((1, ts, width), lambda i, t: (i, t, 0)),
        out_shape=jax.ShapeDtypeStruct((b, s, width), BF16),
        scratch_shapes=[pltpu.VMEM((B_HEADS, B_HEAD_DIM, B_HEAD_DIM), F32)],
        compiler_params=_cparams(("arbitrary", "arbitrary")),
        name="hgrn2_mixer",
    )(zb, zb, zb, zb, lb_logits.astype(F32), out_gain.reshape(1, B_HEAD_DIM).astype(F32))


def _shift_rows(x, prev8, k, row8):
    if k == 0:
        return x
    rolled = pltpu.roll(x, k, axis=0)
    head = jnp.where(row8 < k, pltpu.roll(prev8, k, axis=0), rolled[:8])
    return jnp.concatenate([head, rolled[8:]], axis=0)


def _rglru_kernel(g_ref, x_ref, cw_ref, cb_ref, wr_ref, br_ref, wi_ref, bi_ref, lam_ref, o_ref,
                  xprev_ref, hprev_ref, *, ts):
    @pl.when(pl.program_id(1) == 0)
    def _():
        xprev_ref[...] = jnp.zeros(xprev_ref.shape, F32)
        hprev_ref[...] = jnp.zeros(hprev_ref.shape, F32)

    x = x_ref[0]
    width = x.shape[1]
    prev8 = xprev_ref[...]
    row8 = lax.broadcasted_iota(jnp.int32, (8, width), 0)
    cw = cw_ref[...]
    xc = cb_ref[...] + sum(_shift_rows(x, prev8, k, row8) * cw[CONV_WIDTH - 1 - k:CONV_WIDTH - k, :]
                           for k in range(CONV_WIDTH))
    xprev_ref[...] = x[ts - 8:]

    xb = xc.astype(BF16)
    r = jax.nn.sigmoid(jnp.dot(xb, wr_ref[...], preferred_element_type=F32) + br_ref[...])
    ig = jax.nn.sigmoid(jnp.dot(xb, wi_ref[...], preferred_element_type=F32) + bi_ref[...])
    lam = lam_ref[...]
    softplus_neg = jnp.maximum(-lam, 0.0) + jnp.log(1.0 + jnp.exp(-jnp.abs(lam)))
    a = jnp.exp((-RG_C * r) * softplus_neg)
    u = jnp.sqrt(1.0 - a * a) * (ig * xc)

    pos = lax.broadcasted_iota(jnp.int32, (ts, width), 0) & 7
    d = 1
    while d < 8:
        keep = pos >= d
        u = jnp.where(keep, a * pltpu.roll(u, d, axis=0) + u, u)
        a = jnp.where(keep, a * pltpu.roll(a, d, axis=0), a)
        d *= 2
    carry = hprev_ref[0:1, :]
    hs = []
    for g in range(ts // 8):
        hg = u[8 * g:8 * g + 8] + a[8 * g:8 * g + 8] * carry
        carry = hg[7:8]
        hs.append(hg)
    h = jnp.concatenate(hs, axis=0)
    hprev_ref[...] = jnp.broadcast_to(carry, hprev_ref.shape)

    gt = g_ref[0]
    gelu = 0.5 * gt * (1.0 + jnp.tanh(np.float32(math.sqrt(2.0 / math.pi)) * (gt + 0.044715 * (gt * gt * gt))))
    o_ref[0] = (gelu * h).astype(o_ref.dtype)


def rglru_mixer(zn, conv_w, conv_b, wr_bd, br, wi_bd, bi, lam, *, ts=256):
    b, s, two_c = zn.shape
    c = two_c // 2
    ts = min(ts, s)
    vec = lambda a: a.reshape(1, c).astype(F32)
    const = lambda shape: pl.BlockSpec(shape, lambda i, t: (0, 0))
    return pl.pallas_call(
        functools.partial(_rglru_kernel, ts=ts),
        grid=(b, s // ts),
        in_specs=[pl.BlockSpec((1, ts, c), lambda i, t: (i, t, 0)),
                  pl.BlockSpec((1, ts, c), lambda i, t: (i, t, 1)),
                  const((CONV_WIDTH, c)), const((1, c)), const((c, c)), const((1, c)),
                  const((c, c)), const((1, c)), const((1, c))],
        out_specs=pl.BlockSpec((1, ts, c), lambda i, t: (i, t, 0)),
        out_shape=jax.ShapeDtypeStruct((b, s, c), BF16),
        scratch_shapes=[pltpu.VMEM((8, c), F32), pltpu.VMEM((8, c), F32)],
        compiler_params=_cparams(("arbitrary", "arbitrary")),
        name="rglru_mixer",
    )(zn, zn, conv_w.astype(F32), vec(conv_b), wr_bd, vec(br), wi_bd, vec(bi), vec(lam))


def _block_diag(w):
    n, c, d = w.shape
    eye = jnp.eye(n, dtype=w.dtype)
    return (eye[:, None, :, None] * w[:, :, None, :]).reshape(n * c, n * d)


SWA_BLOCKS_PER_STEP = 8


def _swa_kernel(q_ref, kp_ref, kc_ref, vp_ref, vc_ref, sink_ref, o_ref, *, nblk):
    j = pl.program_id(1)
    row = lax.broadcasted_iota(jnp.int32, (PAIR, BLK), 0)
    col = lax.broadcasted_iota(jnp.int32, (PAIR, BLK), 1)
    bias = jnp.where(row < BLK, jnp.where(row > col, 0.0, NEG_BIG), jnp.where(row - BLK <= col, 0.0, NEG_BIG))
    bias_first = jnp.where(j > 0, bias, jnp.where(row < BLK, NEG_BIG, bias))
    bias = jnp.concatenate([bias] * GROUP, axis=1)
    bias_first = jnp.concatenate([bias_first] * GROUP, axis=1)
    for u in range(nblk):
        outs = []
        for g in range(KV_HEADS):
            rows = slice(g * HEAD_DIM, (g + 1) * HEAD_DIM)
            if u == 0:
                keys = jnp.concatenate([kp_ref[0], kc_ref[0, 0:BLK, :]], axis=0)
                vals = jnp.concatenate([vp_ref[0, rows, :], vc_ref[0, rows, 0:BLK]], axis=1)
            else:
                keys = kc_ref[0, (u - 1) * BLK:(u + 1) * BLK, :]
                vals = vc_ref[0, rows, (u - 1) * BLK:(u + 1) * BLK]
            q = q_ref[0, u, :, g * GROUP * BLK:(g + 1) * GROUP * BLK]
            lg = jnp.dot(keys, q, preferred_element_type=F32) + (bias_first if u == 0 else bias)
            sink = sink_ref[g] * LOG2E
            m = jnp.maximum(jnp.max(lg, axis=0, keepdims=True), sink)
            p = jnp.exp2(lg - m)
            den = jnp.sum(p, axis=0, keepdims=True) + jnp.exp2(sink - m)
            outs.append(jnp.dot(vals, p.astype(BF16), preferred_element_type=F32) / den)
        for pair in range(Q_HEADS // 2):
            g, hh = divmod(2 * pair, GROUP)
            o = outs[g]
            two = jnp.concatenate([o[:, hh * BLK:(hh + 1) * BLK], o[:, (hh + 1) * BLK:(hh + 2) * BLK]], axis=0)
            o_ref[0, u * BLK:(u + 1) * BLK, pair * BLK:(pair + 1) * BLK] = two.T.astype(o_ref.dtype)


def swa_attention(q, k, vt, sinks):
    b, s, _ = k.shape
    nblk = SWA_BLOCKS_PER_STEP
    assert s % (nblk * BLK) == 0
    rows = nblk * BLK
    sink_rows = jnp.repeat(sinks.astype(F32), BLK).reshape(KV_HEADS, 1, GROUP * BLK)
    prev = lambda j: jnp.maximum(nblk * j - 1, 0)
    return pl.pallas_call(
        functools.partial(_swa_kernel, nblk=nblk),
        grid=(b, s // rows),
        in_specs=[
            pl.BlockSpec((1, nblk, BLK, Q_HEADS * BLK), lambda i, j: (i, j, 0, 0)),
            pl.BlockSpec((1, BLK, KV_HEADS * HEAD_DIM), lambda i, j: (i, prev(j), 0)),
            pl.BlockSpec((1, rows, KV_HEADS * HEAD_DIM), lambda i, j: (i, j, 0)),
            pl.BlockSpec((1, KV_HEADS * HEAD_DIM, BLK), lambda i, j: (i, 0, prev(j))),
            pl.BlockSpec((1, KV_HEADS * HEAD_DIM, rows), lambda i, j: (i, 0, j)),
            pl.BlockSpec((KV_HEADS, 1, GROUP * BLK), lambda i, j: (0, 0, 0)),
        ],
        out_specs=pl.BlockSpec((1, rows, Q_HEADS * HEAD_DIM), lambda i, j: (i, j, 0)),
        out_shape=jax.ShapeDtypeStruct((b, s, Q_HEADS * HEAD_DIM), BF16),
        compiler_params=_cparams(("arbitrary", "arbitrary")),
        name="swa_attention",
    )(q, k, k, vt, vt, sink_rows)


def _mlp_kernel(x_ref, oa_ref, ob_ref, wo_ref, g_ref, wu_ref, wd_ref, out_ref, *, hc):
    half = oa_ref.shape[1]
    x1 = (x_ref[...] + jnp.dot(oa_ref[...], wo_ref[:half, :], preferred_element_type=F32)
          + jnp.dot(ob_ref[...], wo_ref[half:, :], preferred_element_type=F32))
    ms = jnp.mean(x1 * x1, axis=-1, keepdims=True)
    hn = ((x1 * lax.rsqrt(ms + RMS_EPS)) * g_ref[...]).astype(BF16)
    out_ref[...] = x1
    hidden = wu_ref.shape[1]
    for c in range(hidden // hc):
        up = jnp.dot(hn, wu_ref[:, c * hc:(c + 1) * hc], preferred_element_type=F32)
        act = jnp.square(jnp.maximum(up, 0.0)).astype(BF16)
        out_ref[...] += jnp.dot(act, wd_ref[c * hc:(c + 1) * hc, :], preferred_element_type=F32)


def out_proj_mlp(x, oa, ob, w_out, gain, w_up, w_down, *, tm=512, hc=1024):
    b, s, d = x.shape
    t = b * s
    tm = min(tm, t)
    half = oa.shape[-1]
    hidden = w_up.shape[1]
    const = lambda shape: pl.BlockSpec(shape, lambda i: (0, 0), pipeline_mode=pl.Buffered(1))
    out = pl.pallas_call(
        functools.partial(_mlp_kernel, hc=hc),
        grid=(t // tm,),
        in_specs=[pl.BlockSpec((tm, d), lambda i: (i, 0)),
                  pl.BlockSpec((tm, half), lambda i: (i, 0)),
                  pl.BlockSpec((tm, half), lambda i: (i, 0)),
                  const((d, d)), const((1, d)), const((d, hidden)), const((hidden, d))],
        out_specs=pl.BlockSpec((tm, d), lambda i: (i, 0)),
        out_shape=jax.ShapeDtypeStruct((t, d), F32),
        compiler_params=_cparams(("arbitrary",)),
        name="out_proj_mlp",
    )(x.reshape(t, d), oa.reshape(t, half), ob.reshape(t, half), w_out.astype(BF16),
      gain.reshape(1, d).astype(F32), w_up.astype(BF16), w_down.astype(BF16))
    return out.reshape(b, s, d)


N_EVEN_T = Q_HEADS * HEAD_DIM + 2 * KV_HEADS * HEAD_DIM + IDX_HEADS * HEAD_DIM + HEAD_DIM + IDX_HEADS
N_ODD_NAT = 2 * Q_HEADS * HEAD_DIM


def layer0_attention(x, pos, inp, j, rope=None):
    cos_t, sin_t = rope if rope is not None else rope_tables_t(pos)
    w_in = inp['even_w_in'][j]
    w_t = w_in[:, :N_EVEN_T].T.astype(BF16)
    w_n = w_in[:, N_EVEN_T:].astype(BF16)
    qg, k, vt, iq, ik, iw, zb = input_projection(
        x, inp['norm_mix_g'][2 * j], w_t, w_n, cos_t, sin_t, inp['a_q_norm_g'][j], inp['a_k_norm_g'][j],
        with_indexer=True)
    return dsa_attention(qg, k, vt, iq, ik, iw), zb


def layer1_mixers(x, pos, inp, j, rope=None):
    cos_t, sin_t = rope if rope is not None else rope_tables_t(pos)
    w_in = inp['odd_w_in'][j]
    w_n = w_in[:, :N_ODD_NAT].astype(BF16)
    w_t = w_in[:, N_ODD_NAT:].T.astype(BF16)
    qg, k, vt, zn = input_projection(
        x, inp['norm_mix_g'][2 * j + 1], w_t, w_n, cos_t, sin_t, inp['d_q_norm_g'][j], inp['d_k_norm_g'][j],
        with_indexer=False)
    o_c = rglru_mixer(zn, inp['c_conv_w'][j], inp['c_conv_b'][j],
                      _block_diag(inp['c_rgate_w'][j]).astype(BF16), inp['c_rgate_b'][j],
                      _block_diag(inp['c_igate_w'][j]).astype(BF16), inp['c_igate_b'][j], inp['c_lambda'][j])
    o_d = swa_attention(qg, k, vt, inp['d_sinks'][j])
    return o_c, o_d


def kernel(x, positions, norm_mix_g, norm_mlp_g, even_w_in, even_w_out, a_q_norm_g, a_k_norm_g, b_lb_logits,
           b_out_norm_g, odd_w_in, odd_w_out, c_conv_w, c_conv_b, c_rgate_w, c_rgate_b, c_igate_w, c_igate_b,
           c_lambda, d_q_norm_g, d_k_norm_g, d_sinks, mlp_w_up, mlp_w_down):
    inp = dict(norm_mix_g=norm_mix_g, even_w_in=even_w_in, a_q_norm_g=a_q_norm_g, a_k_norm_g=a_k_norm_g,
               odd_w_in=odd_w_in, c_conv_w=c_conv_w, c_conv_b=c_conv_b, c_rgate_w=c_rgate_w, c_rgate_b=c_rgate_b,
               c_igate_w=c_igate_w, c_igate_b=c_igate_b, c_lambda=c_lambda, d_q_norm_g=d_q_norm_g,
               d_k_norm_g=d_k_norm_g, d_sinks=d_sinks)
    depth = norm_mix_g.shape[0]
    rope = rope_tables_t(positions)
    for layer in range(depth):
        j = layer // 2
        if layer % 2 == 0:
            o_a, zb = layer0_attention(x, positions, inp, j, rope)
            o_b = hgrn2_mixer(zb, b_lb_logits, b_out_norm_g[j], j)
            x = out_proj_mlp(x, o_a, o_b, even_w_out[j], norm_mlp_g[layer], mlp_w_up[layer], mlp_w_down[layer])
        else:
            o_c, o_d = layer1_mixers(x, positions, inp, j, rope)
            x = out_proj_mlp(x, o_c, o_d, odd_w_out[j], norm_mlp_g[layer], mlp_w_up[layer], mlp_w_down[layer])
    return x
```

```python
import functools
import math

import numpy as np
import jax
import jax.numpy as jnp
from jax import lax
from jax.experimental import pallas as pl
from jax.experimental.pallas import tpu as pltpu

F32 = jnp.float32
BF16 = jnp.bfloat16

HEAD_DIM = 64
ROPE_THETA = 10000.0
RMS_EPS = 1e-6
KV_HEADS = 2
Q_HEADS = 8
GROUP = Q_HEADS // KV_HEADS
IDX_HEADS = 8
DSA_TOPK = 256
BLK = 128
B_HEAD_DIM = 128
B_HEADS = 4
B_CHUNK = 64
B_SUB = 16
CONV_WIDTH = 4
RG_C = 8.0
NEG_BIG = -1e30
LOG2E = math.log2(math.e)

VMEM_LIMIT = 56 * 1024 * 1024


def _cparams(sem):
    return pltpu.CompilerParams(dimension_semantics=sem, vmem_limit_bytes=VMEM_LIMIT)


def _rope_kernel(pos_ref, invf_ref, cos_ref, sin_ref):
    ang = pos_ref[0].astype(F32) * invf_ref[...]
    cos_ref[0] = jnp.cos(ang)
    sin_ref[0] = jnp.sin(ang)


def rope_tables_t(positions):
    b, s = positions.shape
    half = HEAD_DIM // 2
    invf = (1.0 / (ROPE_THETA ** (jnp.arange(0, HEAD_DIM, 2, dtype=F32) / HEAD_DIM))).reshape(half, 1)
    out = jax.ShapeDtypeStruct((b, half, s), F32)
    return pl.pallas_call(
        _rope_kernel,
        grid=(b,),
        in_specs=[pl.BlockSpec((1, 1, s), lambda i: (i, 0, 0)),
                  pl.BlockSpec((half, 1), lambda i: (0, 0))],
        out_specs=[pl.BlockSpec((1, half, s), lambda i: (i, 0, 0))] * 2,
        out_shape=[out, out],
        compiler_params=_cparams(("arbitrary",)),
        name="rope_tables",
    )(positions.reshape(b, 1, s), invf)


def _head_norm_rope_t(blk, gain_col, cos, sin, scale):
    if gain_col is not None:
        ms = jnp.mean(blk * blk, axis=0, keepdims=True)
        blk = (blk * lax.rsqrt(ms + RMS_EPS)) * gain_col
    half = HEAD_DIM // 2
    x1, x2 = blk[:half], blk[half:]
    o1 = x1 * cos - x2 * sin
    o2 = x2 * cos + x1 * sin
    out = jnp.concatenate([o1, o2], axis=0)
    if scale != 1.0:
        out = out * scale
    return out


def _proj_kernel(*refs, tm, with_indexer):
    if with_indexer:
        (x_ref, g_ref, wt_ref, wn_ref, cos_ref, sin_ref, qgain_ref, kgain_ref,
         qg_ref, k_ref, vt_ref, iq_ref, ik_ref, iw_ref, nat_ref) = refs
    else:
        (x_ref, g_ref, wt_ref, wn_ref, cos_ref, sin_ref, qgain_ref, kgain_ref,
         qg_ref, k_ref, vt_ref, nat_ref) = refs
    nq = tm // BLK
    x = x_ref[0]
    ms = jnp.mean(x * x, axis=-1, keepdims=True)
    hn = ((x * lax.rsqrt(ms + RMS_EPS)) * g_ref[...]).astype(BF16)
    zt = lax.dot_general(wt_ref[...], hn, (((1,), (1,)), ((), ())), preferred_element_type=F32)
    nat_ref[0] = jnp.dot(hn, wn_ref[...], preferred_element_type=F32)
    cos = cos_ref[0]
    sin = sin_ref[0]
    qgain = qgain_ref[...]
    kgain = kgain_ref[...]
    zeros64 = jnp.zeros((HEAD_DIM, BLK), BF16)
    r = 0
    for h in range(Q_HEADS):
        g, hh = divmod(h, GROUP)
        q = _head_norm_rope_t(zt[r:r + HEAD_DIM], qgain, cos, sin, HEAD_DIM ** -0.5 * LOG2E).astype(BF16)
        r += HEAD_DIM
        for j in range(nq):
            qg_ref[0, j, g * HEAD_DIM:(g + 1) * HEAD_DIM, h * BLK:(h + 1) * BLK] = q[:, j * BLK:(j + 1) * BLK]
            qg_ref[0, j, (1 - g) * HEAD_DIM:(2 - g) * HEAD_DIM, h * BLK:(h + 1) * BLK] = zeros64
    ks = []
    for g in range(KV_HEADS):
        ks.append(_head_norm_rope_t(zt[r:r + HEAD_DIM], kgain, cos, sin, 1.0))
        r += HEAD_DIM
    k_ref[0] = jnp.concatenate(ks, axis=0).T.astype(BF16)
    vt_ref[0] = zt[r:r + KV_HEADS * HEAD_DIM].astype(BF16)
    r += KV_HEADS * HEAD_DIM
    if with_indexer:
        for h in range(IDX_HEADS):
            iq = _head_norm_rope_t(zt[r:r + HEAD_DIM], None, cos, sin, 1.0).astype(BF16)
            r += HEAD_DIM
            for j in range(nq):
                iq_ref[0, j, 0:HEAD_DIM, h * BLK:(h + 1) * BLK] = iq[:, j * BLK:(j + 1) * BLK]
                iq_ref[0, j, HEAD_DIM:2 * HEAD_DIM, h * BLK:(h + 1) * BLK] = zeros64
        ikt = _head_norm_rope_t(zt[r:r + HEAD_DIM], None, cos, sin, 1.0)
        r += HEAD_DIM
        ik_ref[0] = jnp.concatenate([ikt, jnp.zeros_like(ikt)], axis=0).T.astype(BF16)
        iw_ref[0] = zt[r:r + IDX_HEADS] * (HEAD_DIM ** -0.5 * IDX_HEADS ** -0.5)
        r += IDX_HEADS


def input_projection(x, gain, w_t, w_n, cos_t, sin_t, q_gain, k_gain, *, with_indexer, tm=512):
    b, s, d = x.shape
    tm = min(tm, s)
    nt = w_t.shape[0]
    n_nat = w_n.shape[1]
    nqb = s // BLK
    half = HEAD_DIM // 2
    grid = (b, s // tm)
    in_specs = [
        pl.BlockSpec((1, tm, d), lambda i, j: (i, j, 0)),
        pl.BlockSpec((1, d), lambda i, j: (0, 0)),
        pl.BlockSpec((nt, d), lambda i, j: (0, 0)),
        pl.BlockSpec((d, n_nat), lambda i, j: (0, 0)),
        pl.BlockSpec((1, half, tm), lambda i, j: (i, 0, j)),
        pl.BlockSpec((1, half, tm), lambda i, j: (i, 0, j)),
        pl.BlockSpec((HEAD_DIM, 1), lambda i, j: (0, 0)),
        pl.BlockSpec((HEAD_DIM, 1), lambda i, j: (0, 0)),
    ]
    out_shape = [
        jax.ShapeDtypeStruct((b, nqb, BLK, Q_HEADS * BLK), BF16),
        jax.ShapeDtypeStruct((b, s, KV_HEADS * HEAD_DIM), BF16),
        jax.ShapeDtypeStruct((b, KV_HEADS * HEAD_DIM, s), BF16),
    ]
    out_specs = [
        pl.BlockSpec((1, tm // BLK, BLK, Q_HEADS * BLK), lambda i, j: (i, j, 0, 0)),
        pl.BlockSpec((1, tm, KV_HEADS * HEAD_DIM), lambda i, j: (i, j, 0)),
        pl.BlockSpec((1, KV_HEADS * HEAD_DIM, tm), lambda i, j: (i, 0, j)),
    ]
    if with_indexer:
        out_shape += [
            jax.ShapeDtypeStruct((b, nqb, BLK, IDX_HEADS * BLK), BF16),
            jax.ShapeDtypeStruct((b, s, BLK), BF16),
            jax.ShapeDtypeStruct((b, IDX_HEADS, s), F32),
        ]
        out_specs += [
            pl.BlockSpec((1, tm // BLK, BLK, IDX_HEADS * BLK), lambda i, j: (i, j, 0, 0)),
            pl.BlockSpec((1, tm, BLK), lambda i, j: (i, j, 0)),
            pl.BlockSpec((1, IDX_HEADS, tm), lambda i, j: (i, 0, j)),
        ]
    out_shape.append(jax.ShapeDtypeStruct((b, s, n_nat), F32))
    out_specs.append(pl.BlockSpec((1, tm, n_nat), lambda i, j: (i, j, 0)))
    return pl.pallas_call(
        functools.partial(_proj_kernel, tm=tm, with_indexer=with_indexer),
        grid=grid,
        in_specs=in_specs,
        out_specs=out_specs,
        out_shape=out_shape,
        compiler_params=_cparams(("arbitrary", "arbitrary")),
        name="in_proj_dsa" if with_indexer else "in_proj_swa",
    )(x, gain.reshape(1, d), w_t, w_n, cos_t, sin_t, q_gain.reshape(HEAD_DIM, 1), k_gain.reshape(HEAD_DIM, 1))


PAIR = 2 * BLK
QUAD = 4 * BLK
COUNT_ROWS = 64
PAIRS_PER_TRIP = (8, 4, 2, 1)
FLT_LOWEST = float(np.finfo(np.float32).min)
SEARCH_STEPS = 16
SEARCH_STEPS_PER_TRIP = 4
SAFE_LOGIT_BOUND = 60.0


def _fold8(x):
    return x.reshape(x.shape[0] // 8, 8, x.shape[1])


def _pair_loop(npair, pair_fn, carry):
    start = 0
    for width in PAIRS_PER_TRIP:
        trips = (npair - start) // width

        def body(i, c, start=start, width=width):
            for u in range(width):
                c = pair_fn(start + width * i + u, c)
            return c
        carry = lax.fori_loop(0, trips, body, carry)
        start = start + width * trips
    return carry


def _count(sc_ref, nquad, thr):
    def body(i, acc):
        off = pl.multiple_of(i * QUAD, QUAD)
        hit = jnp.where(sc_ref[pl.ds(off, QUAD), :] >= thr, 1.0, 0.0)
        return acc + jnp.sum(hit.reshape(QUAD // COUNT_ROWS, COUNT_ROWS, BLK), axis=0)
    acc = lax.fori_loop(0, nquad, body, jnp.zeros((COUNT_ROWS, BLK), F32))
    return jnp.sum(acc, axis=0, keepdims=True)


def _resolve_surplus(sc_ref, nquad, thr, excess):
    def smallest_selected():
        def min_body(i, acc):
            off = pl.multiple_of(i * QUAD, QUAD)
            blk = sc_ref[pl.ds(off, QUAD), :]
            return jnp.minimum(acc, jnp.min(_fold8(jnp.where(blk >= thr, blk, jnp.inf)), axis=0))
        low = jnp.min(lax.fori_loop(0, nquad, min_body, jnp.full((8, BLK), jnp.inf, F32)), axis=0, keepdims=True)

        def cnt_body(i, acc):
            off = pl.multiple_of(i * QUAD, QUAD)
            hit = jnp.where(sc_ref[pl.ds(off, QUAD), :] == low, 1.0, 0.0)
            return acc + jnp.sum(hit.reshape(QUAD // COUNT_ROWS, COUNT_ROWS, BLK), axis=0)
        copies = jnp.sum(lax.fori_loop(0, nquad, cnt_body, jnp.zeros((COUNT_ROWS, BLK), F32)), axis=0, keepdims=True)
        return low, copies

    def cond(st):
        ex, _, copies = st
        return jnp.max(jnp.where(ex > copies, 1.0, 0.0)) > 0.0

    def body(st):
        ex, low, copies = st
        whole = ex > copies
        gone = jnp.where(whole, low, jnp.inf)

        def drop_body(i, carry):
            off = pl.multiple_of(i * QUAD, QUAD)
            blk = sc_ref[pl.ds(off, QUAD), :]
            sc_ref[pl.ds(off, QUAD), :] = jnp.where(blk == gone, -jnp.inf, blk)
            return carry
        lax.fori_loop(0, nquad, drop_body, 0)
        ex = jnp.where(whole, ex - copies, ex)
        low, copies = smallest_selected()
        return ex, low, copies

    low, copies = smallest_selected()
    ex, low, copies = lax.while_loop(cond, body, (excess, low, copies))
    surplus = ex > 0.0
    return jnp.where(surplus, low, jnp.inf), jnp.where(surplus, copies - ex, 0.0)


def _search_step(sc_ref, nquad, topk, st):
    lo, hi, c_lo, c_hi, done = st
    guess = lo + (hi - lo) * 0.5
    inside = (guess > lo) & (guess < hi)
    c = _count(sc_ref, nquad, guess)
    move = inside & (done == 0.0)
    up = move & (c >= topk)
    down = move & (c < topk)
    lo = jnp.where(up, guess, lo)
    c_lo = jnp.where(up, c, c_lo)
    hi = jnp.where(down, guess, hi)
    c_hi = jnp.where(down, c, c_hi)
    done = jnp.where(inside & (c_lo != topk), done, 1.0)
    return lo, hi, c_lo, c_hi, done


def _dsa_kernel(q_ref, k_ref, vt_ref, iq_ref, ik_ref, iw_ref, bound_ref, o_ref, sc_ref, lg_ref, acc_ref, mx_ref,
                tri_ref, tie_ref, l_ref, pb_ref, *, topk):
    qi = pl.program_id(1)
    npair = (qi + 2) // 2
    iq = iq_ref[0, 0]
    w = iw_ref[0]
    nquad = (qi + 4) // 4
    row2 = lax.broadcasted_iota(jnp.int32, (PAIR, BLK), 0)
    col2 = lax.broadcasted_iota(jnp.int32, (PAIR, BLK), 1)

    def score_body(pi, carry):
        lo8, hi8 = carry
        off = pl.multiple_of(pi * PAIR, PAIR)
        ikb = ik_ref[0, pl.ds(off, PAIR), :]
        tot = jnp.zeros((PAIR, BLK), F32)
        for h in range(IDX_HEADS):
            sc = jnp.dot(ikb, iq_ref[0, 0, :, h * BLK:(h + 1) * BLK], preferred_element_type=F32)
            tot = tot + jnp.maximum(sc, 0.0) * w[h:h + 1, :]
        causal = (pi * PAIR + row2) <= (qi * BLK + col2)
        sc_ref[pl.ds(off, PAIR), :] = jnp.where(causal, tot, -jnp.inf)
        lo8 = jnp.minimum(lo8, jnp.min(_fold8(jnp.where(causal, tot, jnp.inf)), axis=0))
        hi8 = jnp.maximum(hi8, jnp.max(_fold8(jnp.where(causal, tot, -jnp.inf)), axis=0))
        return lo8, hi8

    lo8, hi8 = _pair_loop(npair, score_body,
                          (jnp.full((8, BLK), jnp.inf, F32), jnp.full((8, BLK), -jnp.inf, F32)))

    @pl.when(2 * npair < 4 * nquad)
    def _():
        sc_ref[pl.ds(pl.multiple_of(npair * PAIR, PAIR), PAIR), :] = jnp.full((PAIR, BLK), -jnp.inf, F32)

    smin = jnp.min(lo8, axis=0, keepdims=True)
    smax = jnp.max(hi8, axis=0, keepdims=True)
    n_valid = (qi * BLK + 1 + lax.broadcasted_iota(jnp.int32, (1, BLK), 1)).astype(F32)
    few = n_valid <= topk
    above = smax + (jnp.abs(smax) * 1e-6 + 1e-30)
    state = (smin, above, n_valid, jnp.zeros((1, BLK), F32), jnp.where(few, 1.0, 0.0))

    def search_body(i, st):
        for _ in range(SEARCH_STEPS_PER_TRIP):
            st = _search_step(sc_ref, nquad, topk, st)
        return st

    lo, _, c_lo, _, _ = lax.fori_loop(0, SEARCH_STEPS // SEARCH_STEPS_PER_TRIP, search_body, state)
    thr = jnp.where(few, FLT_LOWEST, lo)
    excess = jnp.where(few, 0.0, c_lo - topk)

    def logits_of_pair(off, bias, mx):
        kb = k_ref[0, pl.ds(off, PAIR), :]
        parts = []
        for h in range(Q_HEADS):
            lgh = jnp.dot(kb, q_ref[0, 0, :, h * BLK:(h + 1) * BLK], preferred_element_type=F32) + bias
            lg_ref[pl.ds(off, PAIR), h * BLK:(h + 1) * BLK] = lgh
            parts.append(jnp.max(_fold8(lgh), axis=0))
        return jnp.maximum(mx, jnp.concatenate(parts, axis=1))

    mx0 = jnp.full((8, Q_HEADS * BLK), NEG_BIG, F32)

    tie_ref[0:1, :] = jnp.full((1, BLK), jnp.inf, F32)
    tie_ref[1:2, :] = jnp.zeros((1, BLK), F32)

    @pl.when(jnp.max(excess) > 0.0)
    def _():
        low, keep = _resolve_surplus(sc_ref, nquad, thr, excess)
        tie_ref[0:1, :] = low
        tie_ref[1:2, :] = keep

    low = tie_ref[0:1, :]
    keep = tie_ref[1:2, :]
    bound = bound_ref[...]
    acc_ref[...] = jnp.zeros(acc_ref.shape, F32)
    ls0 = jnp.zeros((8, Q_HEADS * BLK), F32)

    def weigh_values(off, prob_of_head, ls):
        sums = []
        for g in range(KV_HEADS):
            pbs = []
            for h in range(g * GROUP, (g + 1) * GROUP):
                p = prob_of_head(h)
                sums.append(jnp.sum(_fold8(p), axis=0))
                pbs.append(p.astype(BF16))
            vt = vt_ref[0, g * HEAD_DIM:(g + 1) * HEAD_DIM, pl.ds(off, PAIR)]
            acc_ref[g] += jnp.dot(vt, jnp.concatenate(pbs, axis=1), preferred_element_type=F32)
        return ls + jnp.concatenate(sums, axis=1)

    two_pass = jnp.max(keep + jnp.where(bound > SAFE_LOGIT_BOUND, 1.0, 0.0)) > 0.0

    @pl.when(jnp.logical_not(two_pass))
    def _():
        def trip(width, start):
            def body(i, ls):
                base = start + width * i
                for u in range(width):
                    off = pl.multiple_of((base + u) * PAIR, PAIR)
                    sc = sc_ref[pl.ds(off, PAIR), :]
                    bias = jnp.where((sc >= thr) & (sc != low), -bound, NEG_BIG)
                    kb = k_ref[0, pl.ds(off, PAIR), :]
                    sums = []
                    for h in range(Q_HEADS):
                        p = jnp.exp2(jnp.dot(kb, q_ref[0, 0, :, h * BLK:(h + 1) * BLK],
                                             preferred_element_type=F32) + bias)
                        sums.append(jnp.sum(_fold8(p), axis=0))
                        pb_ref[u * PAIR:(u + 1) * PAIR, h * BLK:(h + 1) * BLK] = p.astype(BF16)
                    ls = ls + jnp.concatenate(sums, axis=1)
                off0 = pl.multiple_of(base * PAIR, PAIR)
                for g in range(KV_HEADS):
                    vt = vt_ref[0, g * HEAD_DIM:(g + 1) * HEAD_DIM, pl.ds(off0, width * PAIR)]
                    acc_ref[g] += jnp.dot(vt, pb_ref[0:width * PAIR, g * GROUP * BLK:(g + 1) * GROUP * BLK],
                                          preferred_element_type=F32)
                return ls
            return body

        ls, start = ls0, 0
        for width in PAIRS_PER_TRIP:
            trips = (npair - start) // width
            ls = lax.fori_loop(0, trips, trip(width, start), ls)
            start = start + width * trips
        l_ref[...] = ls

    @pl.when(two_pass)
    def _():
        need_rank = jnp.max(keep) > 0.0

        @pl.when(jnp.logical_not(need_rank))
        def _():
            def logit_body(pi, mx):
                off = pl.multiple_of(pi * PAIR, PAIR)
                sc = sc_ref[pl.ds(off, PAIR), :]
                return logits_of_pair(off, jnp.where((sc >= thr) & (sc != low), 0.0, NEG_BIG), mx)
            mx_ref[...] = _pair_loop(npair, logit_body, mx0)

        @pl.when(need_rank)
        def _():
            row = lax.broadcasted_iota(jnp.int32, (PAIR, PAIR), 0)
            col = lax.broadcasted_iota(jnp.int32, (PAIR, PAIR), 1)
            tri_ref[...] = (col < row).astype(BF16)

            def logit_body(pi, carry):
                mx, seen = carry
                off = pl.multiple_of(pi * PAIR, PAIR)
                sc = sc_ref[pl.ds(off, PAIR), :]
                eq = sc == low
                eqf = jnp.where(eq, 1.0, 0.0)
                rank = jnp.dot(tri_ref[...], eqf.astype(BF16), preferred_element_type=F32) + seen
                bias = jnp.where(sc >= thr, 0.0, NEG_BIG)
                bias = jnp.where(eq & (rank >= keep), NEG_BIG, bias)
                return logits_of_pair(off, bias, mx), seen + jnp.sum(eqf, axis=0, keepdims=True)

            mx, _ = _pair_loop(npair, logit_body, (mx0, jnp.zeros((1, BLK), F32)))
            mx_ref[...] = mx

        m = jnp.max(mx_ref[...], axis=0, keepdims=True)

        def pv_body(pi, ls):
            off = pl.multiple_of(pi * PAIR, PAIR)
            return weigh_values(off, lambda h: jnp.exp2(
                lg_ref[pl.ds(off, PAIR), h * BLK:(h + 1) * BLK] - m[:, h * BLK:(h + 1) * BLK]), ls)
        l_ref[...] = _pair_loop(npair, pv_body, ls0)

    l = jnp.sum(l_ref[...], axis=0, keepdims=True)

    for pair in range(Q_HEADS // 2):
        g, hh = divmod(2 * pair, GROUP)
        o = acc_ref[g] / l[:, g * GROUP * BLK:(g + 1) * GROUP * BLK]
        two = jnp.concatenate([o[:, hh * BLK:(hh + 1) * BLK], o[:, (hh + 1) * BLK:(hh + 2) * BLK]], axis=0)
        o_ref[0, :, pair * BLK:(pair + 1) * BLK] = two.T.astype(o_ref.dtype)


def dsa_attention(qg, k, vt, iq, ik, iw, q_gain, k_gain):
    b, s, _ = k.shape
    assert s % QUAD == 0, "the counting loops pad the score tile to whole groups of four key blocks"
    nqb = s // BLK
    topk = min(DSA_TOPK, s // 4)
    bound = (HEAD_DIM * jnp.max(jnp.abs(q_gain)) * jnp.max(jnp.abs(k_gain)) * (HEAD_DIM ** -0.5 * LOG2E) * 1.02)
    bound = jnp.full((1, BLK), bound, F32)
    return pl.pallas_call(
        functools.partial(_dsa_kernel, topk=topk),
        grid=(b, nqb),
        in_specs=[
            pl.BlockSpec((1, 1, BLK, Q_HEADS * BLK), lambda i, j: (i, j, 0, 0)),
            pl.BlockSpec((1, s, KV_HEADS * HEAD_DIM), lambda i, j: (i, 0, 0)),
            pl.BlockSpec((1, KV_HEADS * HEAD_DIM, s), lambda i, j: (i, 0, 0)),
            pl.BlockSpec((1, 1, BLK, IDX_HEADS * BLK), lambda i, j: (i, j, 0, 0)),
            pl.BlockSpec((1, s, BLK), lambda i, j: (i, 0, 0)),
            pl.BlockSpec((1, IDX_HEADS, BLK), lambda i, j: (i, 0, j)),
            pl.BlockSpec((1, BLK), lambda i, j: (0, 0)),
        ],
        out_specs=pl.BlockSpec((1, BLK, Q_HEADS * HEAD_DIM), lambda i, j: (i, j, 0)),
        out_shape=jax.ShapeDtypeStruct((b, s, Q_HEADS * HEAD_DIM), BF16),
        scratch_shapes=[
            pltpu.VMEM((s, BLK), F32),
            pltpu.VMEM((s, Q_HEADS * BLK), F32),
            pltpu.VMEM((KV_HEADS, HEAD_DIM, GROUP * BLK), F32),
            pltpu.VMEM((8, Q_HEADS * BLK), F32),
            pltpu.VMEM((PAIR, PAIR), BF16),
            pltpu.VMEM((8, BLK), F32),
            pltpu.VMEM((8, Q_HEADS * BLK), F32),
            pltpu.VMEM((PAIRS_PER_TRIP[0] * PAIR, Q_HEADS * BLK), BF16),
        ],
        compiler_params=_cparams(("arbitrary", "arbitrary")),
        name="dsa_attention",
    )(qg, k, vt, iq, ik, iw, bound)


def _split3(x):
    hi = x.astype(BF16)
    r1 = x - hi.astype(F32)
    mid = r1.astype(BF16)
    lo = (r1 - mid.astype(F32)).astype(BF16)
    return hi, mid, lo


def _hgrn_chunk(qraw, flog, v, lb, state_t):
    c = B_CHUNK
    q = qraw * jax.nn.sigmoid(qraw)
    f = lb + (1.0 - lb) * jax.nn.sigmoid(flog)
    k = 1.0 - f
    logf = jnp.log(f)
    row = lax.broadcasted_iota(jnp.int32, (c, c), 0)
    lane = lax.broadcasted_iota(jnp.int32, (c, c), 1)
    incl = (lane <= row).astype(BF16)
    b = sum(jnp.dot(incl, part, preferred_element_type=F32) for part in _split3(logf))

    att = jnp.zeros((c, c), F32)
    sub_pos = row & (B_SUB - 1)
    kf = k
    for delta in range(B_SUB):
        if delta > 0:
            kf = pltpu.roll(kf, 1, axis=0) * f
        diag = jnp.sum(q * kf, axis=-1, keepdims=True)
        att = jnp.where((lane == row - delta) & (sub_pos >= delta), diag, att)

    rows = [jnp.zeros((B_SUB, c), F32)]
    for i in range(1, c // B_SUB):
        b_i = b[i * B_SUB - 1:i * B_SUB, :]
        kt = (k * jnp.exp(jnp.minimum(b_i - b, 0.0))).astype(BF16)
        qt = (q[i * B_SUB:(i + 1) * B_SUB] * jnp.exp(b[i * B_SUB:(i + 1) * B_SUB] - b_i)).astype(BF16)
        rows.append(lax.dot_general(qt, kt, (((1,), (1,)), ((), ())), preferred_element_type=F32))
    cross = jnp.concatenate(rows, axis=0)
    att = jnp.where((lane // B_SUB) < (row // B_SUB), cross, att)

    vb = v.astype(BF16)
    o = jnp.dot(att.astype(BF16), vb, preferred_element_type=F32)
    o = o + lax.dot_general((q * jnp.exp(b)).astype(BF16), state_t.astype(BF16), (((1,), (1,)), ((), ())),
                            preferred_element_type=F32)
    b_last = b[c - 1:c, :]
    kd = (k * jnp.exp(b_last - b)).astype(BF16)
    new_state_t = state_t * jnp.exp(b_last) + lax.dot_general(vb, kd, (((0,), (0,)), ((), ())),
                                                              preferred_element_type=F32)
    return o, new_state_t


def _hgrn_kernel(q_ref, f_ref, v_ref, g_ref, lbl_ref, gain_ref, o_ref, state_ref, *, layer, ts):
    @pl.when(pl.program_id(1) == 0)
    def _():
        state_ref[...] = jnp.zeros(state_ref.shape, F32)

    lbl = lbl_ref[...]
    e = jnp.exp(lbl - jnp.max(lbl, axis=0, keepdims=True))
    lb_all = jnp.sum(e[:layer + 1], axis=0, keepdims=True) / jnp.sum(e, axis=0, keepdims=True)
    gain = gain_ref[...]

    def chunk_body(ci, carry):
        r0 = pl.multiple_of(ci * B_CHUNK, B_CHUNK)
        for h in range(B_HEADS):
            cs = slice(h * B_HEAD_DIM, (h + 1) * B_HEAD_DIM)
            o, st = _hgrn_chunk(q_ref[0, pl.ds(r0, B_CHUNK), cs], f_ref[0, pl.ds(r0, B_CHUNK), cs],
                                v_ref[0, pl.ds(r0, B_CHUNK), cs], lb_all[:, cs], state_ref[h])
            state_ref[h] = st
            ms = jnp.mean(o * o, axis=-1, keepdims=True)
            o = (o * lax.rsqrt(ms + RMS_EPS)) * gain
            gate = g_ref[0, pl.ds(r0, B_CHUNK), cs]
            o_ref[0, pl.ds(r0, B_CHUNK), cs] = (o * (gate * jax.nn.sigmoid(gate))).astype(o_ref.dtype)
        return carry

    lax.fori_loop(0, ts // B_CHUNK, chunk_body, 0, unroll=4)


def hgrn2_mixer(zb, lb_logits, out_gain, layer, *, ts=256):
    b, s, _ = zb.shape
    width = B_HEADS * B_HEAD_DIM
    ts = min(ts, s)
    spec = lambda c: pl.BlockSpec((1, ts, width), lambda i, t, c=c: (i, t, c))
    return pl.pallas_call(
        functools.partial(_hgrn_kernel, layer=layer, ts=ts),
        grid=(b, s // ts),
        in_specs=[spec(0), spec(1), spec(2), spec(3),
                  pl.BlockSpec(lb_logits.shape, lambda i, t: (0, 0)),
                  pl.BlockSpec((1, B_HEAD_DIM), lambda i, t: (0, 0))],
        out_specs=pl.BlockSpec((1, ts, width), lambda i, t: (i, t, 0)),
        out_shape=jax.ShapeDtypeStruct((b, s, width), BF16),
        scratch_shapes=[pltpu.VMEM((B_HEADS, B_HEAD_DIM, B_HEAD_DIM), F32)],
        compiler_params=_cparams(("arbitrary", "arbitrary")),
        name="hgrn2_mixer",
    )(zb, zb, zb, zb, lb_logits.astype(F32), out_gain.reshape(1, B_HEAD_DIM).astype(F32))


def _shift_rows(x, prev8, k, row8):
    if k == 0:
        return x
    rolled = pltpu.roll(x, k, axis=0)
    head = jnp.where(row8 < k, pltpu.roll(prev8, k, axis=0), rolled[:8])
    return jnp.concatenate([head, rolled[8:]], axis=0)


def _rglru_kernel(g_ref, x_ref, cw_ref, cb_ref, wr_ref, br_ref, wi_ref, bi_ref, lam_ref, o_ref,
                  xprev_ref, hprev_ref, *, ts):
    @pl.when(pl.program_id(1) == 0)
    def _():
        xprev_ref[...] = jnp.zeros(xprev_ref.shape, F32)
        hprev_ref[...] = jnp.zeros(hprev_ref.shape, F32)

    x = x_ref[0]
    width = x.shape[1]
    prev8 = xprev_ref[...]
    row8 = lax.broadcasted_iota(jnp.int32, (8, width), 0)
    cw = cw_ref[...]
    xc = cb_ref[...] + sum(_shift_rows(x, prev8, k, row8) * cw[CONV_WIDTH - 1 - k:CONV_WIDTH - k, :]
                           for k in range(CONV_WIDTH))
    xprev_ref[...] = x[ts - 8:]

    xb = xc.astype(BF16)
    r = jax.nn.sigmoid(jnp.dot(xb, wr_ref[...], preferred_element_type=F32) + br_ref[...])
    ig = jax.nn.sigmoid(jnp.dot(xb, wi_ref[...], preferred_element_type=F32) + bi_ref[...])
    lam = lam_ref[...]
    softplus_neg = jnp.maximum(-lam, 0.0) + jnp.log(1.0 + jnp.exp(-jnp.abs(lam)))
    a = jnp.exp((-RG_C * r) * softplus_neg)
    u = jnp.sqrt(1.0 - a * a) * (ig * xc)

    pos = lax.broadcasted_iota(jnp.int32, (ts, width), 0) & 7
    d = 1
    while d < 8:
        keep = pos >= d
        u = jnp.where(keep, a * pltpu.roll(u, d, axis=0) + u, u)
        a = jnp.where(keep, a * pltpu.roll(a, d, axis=0), a)
        d *= 2
    carry = hprev_ref[0:1, :]
    hs = []
    for g in range(ts // 8):
        hg = u[8 * g:8 * g + 8] + a[8 * g:8 * g + 8] * carry
        carry = hg[7:8]
        hs.append(hg)
    h = jnp.concatenate(hs, axis=0)
    hprev_ref[...] = jnp.broadcast_to(carry, hprev_ref.shape)

    gt = g_ref[0]
    gelu = 0.5 * gt * (1.0 + jnp.tanh(np.float32(math.sqrt(2.0 / math.pi)) * (gt + 0.044715 * (gt * gt * gt))))
    o_ref[0] = (gelu * h).astype(o_ref.dtype)


def rglru_mixer(zn, conv_w, conv_b, wr_bd, br, wi_bd, bi, lam, *, ts=256):
    b, s, two_c = zn.shape
    c = two_c // 2
    ts = min(ts, s)
    vec = lambda a: a.reshape(1, c).astype(F32)
    const = lambda shape: pl.BlockSpec(shape, lambda i, t: (0, 0))
    return pl.pallas_call(
        functools.partial(_rglru_kernel, ts=ts),
        grid=(b, s // ts),
        in_specs=[pl.BlockSpec((1, ts, c), lambda i, t: (i, t, 0)),
                  pl.BlockSpec((1, ts, c), lambda i, t: (i, t, 1)),
                  const((CONV_WIDTH, c)), const((1, c)), const((c, c)), const((1, c)),
                  const((c, c)), const((1, c)), const((1, c))],
        out_specs=pl.BlockSpec((1, ts, c), lambda i, t: (i, t, 0)),
        out_shape=jax.ShapeDtypeStruct((b, s, c), BF16),
        scratch_shapes=[pltpu.VMEM((8, c), F32), pltpu.VMEM((8, c), F32)],
        compiler_params=_cparams(("arbitrary", "arbitrary")),
        name="rglru_mixer",
    )(zn, zn, conv_w.astype(F32), vec(conv_b), wr_bd, vec(br), wi_bd, vec(bi), vec(lam))


def _block_diag(w):
    n, c, d = w.shape
    eye = jnp.eye(n, dtype=w.dtype)
    return (eye[:, None, :, None] * w[:, :, None, :]).reshape(n * c, n * d)


SWA_BLOCKS_PER_STEP = 4


def _swa_kernel(q_ref, kp_ref, kc_ref, vp_ref, vc_ref, sink_ref, o_ref, *, nblk):
    j = pl.program_id(1)
    row = lax.broadcasted_iota(jnp.int32, (PAIR, BLK), 0)
    col = lax.broadcasted_iota(jnp.int32, (PAIR, BLK), 1)
    bias = jnp.where(row < BLK, jnp.where(row > col, 0.0, NEG_BIG), jnp.where(row - BLK <= col, 0.0, NEG_BIG))
    bias_first = jnp.where(j > 0, bias, jnp.where(row < BLK, NEG_BIG, bias))
    bias = jnp.concatenate([bias] * GROUP, axis=1)
    bias_first = jnp.concatenate([bias_first] * GROUP, axis=1)
    for u in range(nblk):
        outs = []
        for g in range(KV_HEADS):
            rows = slice(g * HEAD_DIM, (g + 1) * HEAD_DIM)
            if u == 0:
                keys = jnp.concatenate([kp_ref[0], kc_ref[0, 0:BLK, :]], axis=0)
                vals = jnp.concatenate([vp_ref[0, rows, :], vc_ref[0, rows, 0:BLK]], axis=1)
            else:
                keys = kc_ref[0, (u - 1) * BLK:(u + 1) * BLK, :]
                vals = vc_ref[0, rows, (u - 1) * BLK:(u + 1) * BLK]
            q = q_ref[0, u, :, g * GROUP * BLK:(g + 1) * GROUP * BLK]
            lg = jnp.dot(keys, q, preferred_element_type=F32) + (bias_first if u == 0 else bias)
            sink = sink_ref[g] * LOG2E
            m = jnp.maximum(jnp.max(lg, axis=0, keepdims=True), sink)
            p = jnp.exp2(lg - m)
            den = jnp.sum(p, axis=0, keepdims=True) + jnp.exp2(sink - m)
            outs.append(jnp.dot(vals, p.astype(BF16), preferred_element_type=F32) / den)
        for pair in range(Q_HEADS // 2):
            g, hh = divmod(2 * pair, GROUP)
            o = outs[g]
            two = jnp.concatenate([o[:, hh * BLK:(hh + 1) * BLK], o[:, (hh + 1) * BLK:(hh + 2) * BLK]], axis=0)
            o_ref[0, u * BLK:(u + 1) * BLK, pair * BLK:(pair + 1) * BLK] = two.T.astype(o_ref.dtype)


def swa_attention(q, k, vt, sinks):
    b, s, _ = k.shape
    nblk = SWA_BLOCKS_PER_STEP
    assert s % (nblk * BLK) == 0
    rows = nblk * BLK
    sink_rows = jnp.repeat(sinks.astype(F32), BLK).reshape(KV_HEADS, 1, GROUP * BLK)
    prev = lambda j: jnp.maximum(nblk * j - 1, 0)
    return pl.pallas_call(
        functools.partial(_swa_kernel, nblk=nblk),
        grid=(b, s // rows),
        in_specs=[
            pl.BlockSpec((1, nblk, BLK, Q_HEADS * BLK), lambda i, j: (i, j, 0, 0)),
            pl.BlockSpec((1, BLK, KV_HEADS * HEAD_DIM), lambda i, j: (i, prev(j), 0)),
            pl.BlockSpec((1, rows, KV_HEADS * HEAD_DIM), lambda i, j: (i, j, 0)),
            pl.BlockSpec((1, KV_HEADS * HEAD_DIM, BLK), lambda i, j: (i, 0, prev(j))),
            pl.BlockSpec((1, KV_HEADS * HEAD_DIM, rows), lambda i, j: (i, 0, j)),
            pl.BlockSpec((KV_HEADS, 1, GROUP * BLK), lambda i, j: (0, 0, 0)),
        ],
        out_specs=pl.BlockSpec((1, rows, Q_HEADS * HEAD_DIM), lambda i, j: (i, j, 0)),
        out_shape=jax.ShapeDtypeStruct((b, s, Q_HEADS * HEAD_DIM), BF16),
        compiler_params=_cparams(("arbitrary", "arbitrary")),
        name="swa_attention",
    )(q, k, k, vt, vt, sink_rows)


def _mlp_kernel(x_ref, oa_ref, ob_ref, wo_ref, g_ref, wu_ref, wd_ref, out_ref, *, hc):
    half = oa_ref.shape[1]
    x1 = (x_ref[...] + jnp.dot(oa_ref[...], wo_ref[:half, :], preferred_element_type=F32)
          + jnp.dot(ob_ref[...], wo_ref[half:, :], preferred_element_type=F32))
    ms = jnp.mean(x1 * x1, axis=-1, keepdims=True)
    hn = ((x1 * lax.rsqrt(ms + RMS_EPS)) * g_ref[...]).astype(BF16)
    out_ref[...] = x1
    hidden = wu_ref.shape[1]
    for c in range(hidden // hc):
        up = jnp.dot(hn, wu_ref[:, c * hc:(c + 1) * hc], preferred_element_type=F32)
        act = jnp.square(jnp.maximum(up, 0.0)).astype(BF16)
        out_ref[...] += jnp.dot(act, wd_ref[c * hc:(c + 1) * hc, :], preferred_element_type=F32)


def out_proj_mlp(x, oa, ob, w_out, gain, w_up, w_down, *, tm=512, hc=1024):
    b, s, d = x.shape
    t = b * s
    tm = min(tm, t)
    half = oa.shape[-1]
    hidden = w_up.shape[1]
    const = lambda shape: pl.BlockSpec(shape, lambda i: (0, 0), pipeline_mode=pl.Buffered(1))
    out = pl.pallas_call(
        functools.partial(_mlp_kernel, hc=hc),
        grid=(t // tm,),
        in_specs=[pl.BlockSpec((tm, d), lambda i: (i, 0)),
                  pl.BlockSpec((tm, half), lambda i: (i, 0)),
                  pl.BlockSpec((tm, half), lambda i: (i, 0)),
                  const((d, d)), const((1, d)), const((d, hidden)), const((hidden, d))],
        out_specs=pl.BlockSpec((tm, d), lambda i: (i, 0)),
        out_shape=jax.ShapeDtypeStruct((t, d), F32),
        compiler_params=_cparams(("arbitrary",)),
        name="out_proj_mlp",
    )(x.reshape(t, d), oa.reshape(t, half), ob.reshape(t, half), w_out.astype(BF16),
      gain.reshape(1, d).astype(F32), w_up.astype(BF16), w_down.astype(BF16))
    return out.reshape(b, s, d)


N_EVEN_T = Q_HEADS * HEAD_DIM + 2 * KV_HEADS * HEAD_DIM + IDX_HEADS * HEAD_DIM + HEAD_DIM + IDX_HEADS
N_ODD_NAT = 2 * Q_HEADS * HEAD_DIM


def layer0_attention(x, pos, inp, j, rope=None):
    cos_t, sin_t = rope if rope is not None else rope_tables_t(pos)
    w_in = inp['even_w_in'][j]
    w_t = w_in[:, :N_EVEN_T].T.astype(BF16)
    w_n = w_in[:, N_EVEN_T:].astype(BF16)
    qg, k, vt, iq, ik, iw, zb = input_projection(
        x, inp['norm_mix_g'][2 * j], w_t, w_n, cos_t, sin_t, inp['a_q_norm_g'][j], inp['a_k_norm_g'][j],
        with_indexer=True)
    return dsa_attention(qg, k, vt, iq, ik, iw, inp['a_q_norm_g'][j], inp['a_k_norm_g'][j]), zb


def layer1_mixers(x, pos, inp, j, rope=None):
    cos_t, sin_t = rope if rope is not None else rope_tables_t(pos)
    w_in = inp['odd_w_in'][j]
    w_n = w_in[:, :N_ODD_NAT].astype(BF16)
    w_t = w_in[:, N_ODD_NAT:].T.astype(BF16)
    qg, k, vt, zn = input_projection(
        x, inp['norm_mix_g'][2 * j + 1], w_t, w_n, cos_t, sin_t, inp['d_q_norm_g'][j], inp['d_k_norm_g'][j],
        with_indexer=False)
    o_c = rglru_mixer(zn, inp['c_conv_w'][j], inp['c_conv_b'][j],
                      _block_diag(inp['c_rgate_w'][j]).astype(BF16), inp['c_rgate_b'][j],
                      _block_diag(inp['c_igate_w'][j]).astype(BF16), inp['c_igate_b'][j], inp['c_lambda'][j])
    o_d = swa_attention(qg, k, vt, inp['d_sinks'][j])
    return o_c, o_d


def kernel(x, positions, norm_mix_g, norm_mlp_g, even_w_in, even_w_out, a_q_norm_g, a_k_norm_g, b_lb_logits,
           b_out_norm_g, odd_w_in, odd_w_out, c_conv_w, c_conv_b, c_rgate_w, c_rgate_b, c_igate_w, c_igate_b,
           c_lambda, d_q_norm_g, d_k_norm_g, d_sinks, mlp_w_up, mlp_w_down):
    inp = dict(norm_mix_g=norm_mix_g, even_w_in=even_w_in, a_q_norm_g=a_q_norm_g, a_k_norm_g=a_k_norm_g,
               odd_w_in=odd_w_in, c_conv_w=c_conv_w, c_conv_b=c_conv_b, c_rgate_w=c_rgate_w, c_rgate_b=c_rgate_b,
               c_igate_w=c_igate_w, c_igate_b=c_igate_b, c_lambda=c_lambda, d_q_norm_g=d_q_norm_g,
               d_k_norm_g=d_k_norm_g, d_sinks=d_sinks)
    depth = norm_mix_g.shape[0]
    rope = rope_tables_t(positions)
    for layer in range(depth):
        j = layer // 2
        if layer % 2 == 0:
            o_a, zb = layer0_attention(x, positions, inp, j, rope)
            o_b = hgrn2_mixer(zb, b_lb_logits, b_out_norm_g[j], j)
            x = out_proj_mlp(x, o_a, o_b, even_w_out[j], norm_mlp_g[layer], mlp_w_up[layer], mlp_w_down[layer])
        else:
            o_c, o_d = layer1_mixers(x, positions, inp, j, rope)
            x = out_proj_mlp(x, o_c, o_d, odd_w_out[j], norm_mlp_g[layer], mlp_w_up[layer], mlp_w_down[layer])
    return x
```
